```python
import jax, jax.numpy as jnp
from jax import lax
import numpy as np

D_MODEL = 1024
BATCH = 8
SEQ = 2048
DEPTH = 4

MIX_WIDTH = D_MODEL
RWKV_WIDTH = MIX_WIDTH // 2
RWKV_HEAD = 64
RWKV_HEADS = RWKV_WIDTH // RWKV_HEAD
DECAY_RANK = 64
ICLR_RANK = 64
VRES_RANK = 32
GATE_RANK = 128
GN_EPS = 64e-5

GDN_WIDTH = MIX_WIDTH - RWKV_WIDTH
GDN_HEAD = 128
GDN_HEADS = GDN_WIDTH // GDN_HEAD
CONV_WIDTH = 4
CHUNK = 64

D_FF = 4 * D_MODEL
NORM_EPS = 1e-5
GATED_NORM_EPS = 1e-6
L2_EPS = 1e-6

RWKV_COLS = 3 * RWKV_WIDTH + DECAY_RANK + ICLR_RANK + GATE_RANK
GDN_COLS = 4 * GDN_WIDTH + 2 * GDN_HEADS
IN_COLS = RWKV_COLS + GDN_COLS
RWKV_SPLITS = (RWKV_WIDTH, 2 * RWKV_WIDTH, 3 * RWKV_WIDTH,
               3 * RWKV_WIDTH + DECAY_RANK, 3 * RWKV_WIDTH + DECAY_RANK + ICLR_RANK)

kernel_name = "hybrid_rwkv7_gdn_trunk"


def rmsnorm(x, g, eps=NORM_EPS):
    xf = x.astype(jnp.float32)
    y = xf * lax.rsqrt(jnp.mean(xf * xf, axis=-1, keepdims=True) + eps)
    return (y * g.astype(jnp.float32)).astype(x.dtype)


def l2norm(x):
    return x * lax.rsqrt(jnp.sum(x * x, axis=-1, keepdims=True) + L2_EPS)


def token_shift(p):
    return jnp.pad(p, ((0, 0), (1, 0), (0, 0)))[:, :-1]


def causal_depthwise_conv(x, w):
    K = w.shape[0]
    T = x.shape[1]
    xp = jnp.pad(x, ((0, 0), (K - 1, 0), (0, 0)))
    return sum(xp[:, j:j + T] * w[j] for j in range(K))


def wkv7_scan(r, w, k, v, kk, a):
    B, T, H, N = r.shape

    def step(S, inp):
        r_t, w_t, k_t, v_t, kk_t, a_t = inp
        s_kk = jnp.einsum('bhvk,bhk->bhv', S, kk_t)
        S = (S * w_t[:, :, None, :]
             - s_kk[..., None] * (kk_t * a_t)[:, :, None, :]
             + v_t[..., None] * k_t[:, :, None, :])
        return S, jnp.einsum('bhvk,bhk->bhv', S, r_t)

    xs = tuple(jnp.moveaxis(t, 1, 0) for t in (r, w, k, v, kk, a))
    S0 = jnp.zeros((B, H, N, N), jnp.float32)
    _, y = lax.scan(step, S0, xs)
    return jnp.moveaxis(y, 0, 1)


def gated_delta_chunked(q, k, v, g, beta):
    B, T, H, Dk = q.shape
    Dv = v.shape[-1]
    N = T // CHUNK

    def chunks(t):
        t = t.reshape((B, N, CHUNK, H) + t.shape[3:])
        return jnp.moveaxis(t, 3, 1)

    q, k, v, g, beta = (chunks(t) for t in (q, k, v, g, beta))
    g = jnp.cumsum(g, axis=-1)
    idx = jnp.arange(CHUNK)
    causal = idx[:, None] >= idx[None, :]
    strict = idx[:, None] > idx[None, :]
    decay = jnp.exp(jnp.where(causal, g[..., :, None] - g[..., None, :], -jnp.inf))
    kb = k * beta[..., None]
    vb = v * beta[..., None]
    lower = jnp.where(strict, jnp.einsum('bhncd,bhnsd->bhncs', kb, k) * decay, 0.0)
    a_mat = lower + jnp.eye(CHUNK, dtype=lower.dtype)
    rhs = jnp.concatenate([vb, kb * jnp.exp(g)[..., None]], axis=-1)
    sol = lax.linalg.triangular_solve(a_mat, rhs, left_side=True, lower=True, unit_diagonal=True)
    u, w_cum = sol[..., :Dv], sol[..., Dv:]
    a_qk = jnp.einsum('bhncd,bhnsd->bhncs', q, k) * decay

    def step(S, inp):
        q_c, k_c, u_c, w_c, g_c, a_c = inp
        v_new = u_c - jnp.einsum('bhcd,bhde->bhce', w_c, S)
        o = (jnp.einsum('bhcd,bhde->bhce', q_c * jnp.exp(g_c)[..., None], S)
             + jnp.einsum('bhcs,bhse->bhce', a_c, v_new))
        g_last = g_c[..., -1:]
        S = (S * jnp.exp(g_last)[..., None]
             + jnp.einsum('bhcd,bhce->bhde', k_c * jnp.exp(g_last - g_c)[..., None], v_new))
        return S, o

    xs = tuple(jnp.moveaxis(t, 2, 0) for t in (q, k, u, w_cum, g, a_qk))
    S0 = jnp.zeros((B, H, Dk, Dv), jnp.float32)
    _, o = lax.scan(step, S0, xs)
    o = jnp.moveaxis(o, 0, 2)
    return jnp.moveaxis(o, 1, 3).reshape(B, T, H, Dv)


def rwkv7_group(p, mu, w_up, w0, a_up, a0, g_up, k_k, k_a, r_k, ln_g, ln_b, v_first, vres):
    B, T, _ = p.shape
    p = p + (token_shift(p) - p) * mu
    r, k, v, wd, ad, gd = jnp.split(p, RWKV_SPLITS, axis=-1)
    w_raw = -jax.nn.softplus(-(w0 + jnp.tanh(wd) @ w_up)) - 0.5
    decay = jnp.exp(-jnp.exp(w_raw))
    a = jax.nn.sigmoid(a0 + ad @ a_up)
    gate = jax.nn.sigmoid(gd) @ g_up
    if vres is None:
        v_first = v
    else:
        v_down, v_upw, v_bias = vres
        v = v + (v_first - v) * jax.nn.sigmoid(v_bias + (v @ v_down) @ v_upw)
    heads = lambda t: t.reshape(B, T, RWKV_HEADS, RWKV_HEAD)
    kk = l2norm(heads(k * k_k))
    k = k * (1.0 + (a - 1.0) * k_a)
    rh, kh, vh, ah, wh = (heads(t) for t in (r, k, v, a, decay))
    y = wkv7_scan(rh, wh, kh, vh, kk, ah)
    mean = jnp.mean(y, axis=-1, keepdims=True)
    var = jnp.mean(jnp.square(y - mean), axis=-1, keepdims=True)
    y = ((y - mean) * lax.rsqrt(var + GN_EPS)).reshape(B, T, RWKV_WIDTH) * ln_g + ln_b
    bonus = jnp.sum(rh * kh * r_k, axis=-1, keepdims=True) * vh
    return (y + bonus.reshape(B, T, RWKV_WIDTH)) * gate, v_first


def gdn_group(p, conv_w, a_log, dt_bias, norm_g):
    B, T, _ = p.shape
    W = GDN_WIDTH
    qkv = jax.nn.silu(causal_depthwise_conv(p[..., :3 * W], conv_w))
    z = p[..., 3 * W:4 * W].reshape(B, T, GDN_HEADS, GDN_HEAD)
    alpha = p[..., 4 * W:4 * W + GDN_HEADS]
    beta = jax.nn.sigmoid(p[..., 4 * W + GDN_HEADS:])
    q, k, v = (t.reshape(B, T, GDN_HEADS, GDN_HEAD) for t in jnp.split(qkv, 3, axis=-1))
    q = l2norm(q) * (GDN_HEAD ** -0.5)
    k = l2norm(k)
    g = -jnp.exp(a_log) * jax.nn.softplus(alpha + dt_bias)
    o = gated_delta_chunked(q, k, v, g, beta)
    o = o * lax.rsqrt(jnp.mean(o * o, axis=-1, keepdims=True) + GATED_NORM_EPS) * norm_g * jax.nn.silu(z)
    return o.reshape(B, T, W)


def setup_inputs(seed: int = 0) -> dict:
    key = jax.random.key(seed)
    ks = iter(jax.random.split(key, 32))
    nrm = lambda shape, scale: jax.random.normal(next(ks), shape, jnp.float32) * scale
    uni = lambda shape, lo, hi: jax.random.uniform(next(ks), shape, jnp.float32, lo, hi)
    L = DEPTH
    x = nrm((BATCH, SEQ, D_MODEL), 1.0)
    norm1_g = 1.0 + nrm((L, D_MODEL), 0.02)
    w_in = nrm((L, D_MODEL, IN_COLS), D_MODEL ** -0.5)
    shift_mu = uni((L, RWKV_COLS), 0.0, 1.0)
    rw_w_up = nrm((L, DECAY_RANK, RWKV_WIDTH), 0.1 * DECAY_RANK ** -0.5)
    rw_w0 = jnp.tile(jnp.linspace(-6.0, 1.0, RWKV_HEAD, dtype=jnp.float32), RWKV_HEADS)[None] + nrm((L, RWKV_WIDTH), 0.1)
    rw_a_up = nrm((L, ICLR_RANK, RWKV_WIDTH), ICLR_RANK ** -0.5)
    rw_a0 = nrm((L, RWKV_WIDTH), 0.1)
    rw_g_up = nrm((L, GATE_RANK, RWKV_WIDTH), GATE_RANK ** -0.5)
    rw_k_k = 0.85 + nrm((L, RWKV_WIDTH), 0.02)
    rw_k_a = 1.0 + nrm((L, RWKV_WIDTH), 0.02)
    rw_r_k = nrm((L, RWKV_HEADS, RWKV_HEAD), 0.1)
    rw_ln_g = 1.0 + nrm((L, RWKV_WIDTH), 0.02)
    rw_ln_b = nrm((L, RWKV_WIDTH), 0.02)
    rw_vres_down = nrm((L - 1, RWKV_WIDTH, VRES_RANK), RWKV_WIDTH ** -0.5)
    rw_vres_up = nrm((L - 1, VRES_RANK, RWKV_WIDTH), VRES_RANK ** -0.5)
    rw_vres_b = nrm((L - 1, RWKV_WIDTH), 0.1)
    gdn_conv = nrm((L, CONV_WIDTH, 3 * GDN_WIDTH), CONV_WIDTH ** -0.5)
    gdn_A_log = jnp.log(uni((L, GDN_HEADS), 1.0, 16.0))
    dt = jnp.exp(uni((L, GDN_HEADS), float(np.log(1e-3)), float(np.log(1e-1))))
    gdn_dt_bias = dt + jnp.log(-jnp.expm1(-dt))
    gdn_norm_g = 1.0 + nrm((L, GDN_HEAD), 0.02)
    w_out = nrm((L, MIX_WIDTH, D_MODEL), MIX_WIDTH ** -0.5)
    norm2_g = 1.0 + nrm((L, D_MODEL), 0.02)
    mlp_up = nrm((L, D_MODEL, D_FF), D_MODEL ** -0.5)
    mlp_down = nrm((L, D_FF, D_MODEL), D_FF ** -0.5)
    final_g = 1.0 + nrm((D_MODEL,), 0.02)
    return {"x": x, "norm1_g": norm1_g, "w_in": w_in, "shift_mu": shift_mu,
            "rw_w_up": rw_w_up, "rw_w0": rw_w0, "rw_a_up": rw_a_up, "rw_a0": rw_a0,
            "rw_g_up": rw_g_up, "rw_k_k": rw_k_k, "rw_k_a": rw_k_a, "rw_r_k": rw_r_k,
            "rw_ln_g": rw_ln_g, "rw_ln_b": rw_ln_b, "rw_vres_down": rw_vres_down,
            "rw_vres_up": rw_vres_up, "rw_vres_b": rw_vres_b, "gdn_conv": gdn_conv,
            "gdn_A_log": gdn_A_log, "gdn_dt_bias": gdn_dt_bias, "gdn_norm_g": gdn_norm_g,
            "w_out": w_out, "norm2_g": norm2_g, "mlp_up": mlp_up, "mlp_down": mlp_down,
            "final_g": final_g}


def reference(x, norm1_g, w_in, shift_mu, rw_w_up, rw_w0, rw_a_up, rw_a0, rw_g_up,
              rw_k_k, rw_k_a, rw_r_k, rw_ln_g, rw_ln_b, rw_vres_down, rw_vres_up, rw_vres_b,
              gdn_conv, gdn_A_log, gdn_dt_bias, gdn_norm_g, w_out, norm2_g, mlp_up, mlp_down,
              final_g):
    h = x
    v_first = None
    for l in range(DEPTH):
        xn = rmsnorm(h, norm1_g[l])
        p = jnp.einsum('btd,dc->btc', xn, w_in[l]).astype(jnp.float32)
        vres = None if l == 0 else (rw_vres_down[l - 1], rw_vres_up[l - 1], rw_vres_b[l - 1])
        y_a, v_first = rwkv7_group(p[..., :RWKV_COLS], shift_mu[l], rw_w_up[l], rw_w0[l],
                                   rw_a_up[l], rw_a0[l], rw_g_up[l], rw_k_k[l], rw_k_a[l],
                                   rw_r_k[l], rw_ln_g[l], rw_ln_b[l], v_first, vres)
        y_b = gdn_group(p[..., RWKV_COLS:], gdn_conv[l], gdn_A_log[l], gdn_dt_bias[l], gdn_norm_g[l])
        mix = jnp.concatenate([y_a, y_b], axis=-1).astype(h.dtype)
        h = h + jnp.einsum('btc,cd->btd', mix, w_out[l])
        xn = rmsnorm(h, norm2_g[l])
        hid = jnp.square(jax.nn.relu(jnp.einsum('btd,df->btf', xn, mlp_up[l])))
        h = h + jnp.einsum('btf,fd->btd', hid, mlp_down[l])
    return rmsnorm(h, final_g)
```

```python
import functools

import jax
import jax.numpy as jnp
from jax import lax
from jax.experimental import pallas as pl
from jax.experimental.pallas import tpu as pltpu

F32 = jnp.float32
BF16 = jnp.bfloat16

D_MODEL = 1024
RWKV_WIDTH = 512
RWKV_HEAD = 64
DECAY_RANK = 64
ICLR_RANK = 64
GATE_RANK = 128
GN_EPS = 64e-5
GDN_WIDTH = 512
GDN_HEAD = 128
GDN_HEADS = 4
CONV_WIDTH = 4
CHUNK = 64
D_FF = 4 * D_MODEL
NORM_EPS = 1e-5
GATED_NORM_EPS = 1e-6
L2_EPS = 1e-6
RWKV_COLS = 3 * RWKV_WIDTH + DECAY_RANK + ICLR_RANK + GATE_RANK
GDN_MAIN_COLS = 4 * GDN_WIDTH
AB_COLS = 2 * GDN_HEADS

LANES = 128
SUBLANES = 8
VMEM_LIMIT_BYTES = 56 * 1024 * 1024

ROW_TILE = 512
SEQ_TILE = 256

NN = (((1,), (0,)), ((), ()))
NT = (((1,), (1,)), ((), ()))
BNN = (((2,), (1,)), ((0,), (0,)))
BNT = (((2,), (2,)), ((0,), (0,)))
BTN = (((1,), (1,)), ((0,), (0,)))


def _mm(a, b, dims=NN):
    return lax.dot_general(a.astype(BF16), b.astype(BF16), dims, preferred_element_type=F32)


def _split(x):
    hi = x.astype(BF16)
    lo = (x - hi.astype(F32)).astype(BF16)
    return hi, lo


def _mm3(a, b, dims=NN):
    ah, al = _split(a)
    bh, bl = _split(b)
    d = lambda x, y: lax.dot_general(x, y, dims, preferred_element_type=F32)
    return d(ah, bh) + (d(ah, bl) + d(al, bh))


def _mm_exact_lhs(a_bf16, b, dims=NN):
    b1 = b.astype(BF16)
    r1 = b - b1.astype(F32)
    b2 = r1.astype(BF16)
    b3 = (r1 - b2.astype(F32)).astype(BF16)
    d = lambda y: lax.dot_general(a_bf16, y, dims, preferred_element_type=F32)
    return d(b1) + (d(b2) + d(b3))


def _rmsnorm(x, g, eps):
    return x * lax.rsqrt(jnp.mean(x * x, axis=-1, keepdims=True) + eps) * g


def _tri_masks(c):
    row = lax.broadcasted_iota(jnp.int32, (c, c), 0)
    col = lax.broadcasted_iota(jnp.int32, (c, c), 1)
    return row, col


def _unit_lower_inverse(low, row, col):
    c = low.shape[-1]
    eye = (row == col).astype(F32)
    same = lambda s: (row ^ col) < s
    t = eye - jnp.where(same(2), low, 0.0)
    s = 2
    while s < c:
        e = jnp.where(same(2 * s) & jnp.logical_not(same(s)), low, 0.0)
        t = t - _mm3(_mm3(t, e, BNN), t, BNN)
        s *= 2
    return t


def _inproj_kernel(x_ref, g_ref, wr_ref, wg_ref, wab_ref, pr_ref, pg_ref, pab_ref):
    xb = _rmsnorm(x_ref[...], g_ref[...], NORM_EPS).astype(BF16)
    pr_ref[...] = jnp.dot(xb, wr_ref[...], preferred_element_type=F32)
    pg_ref[...] = jnp.dot(xb, wg_ref[...], preferred_element_type=F32)
    pab_ref[...] = jnp.dot(xb, wab_ref[...], preferred_element_type=F32)[:, :AB_COLS]


def _const_spec(shape):
    nd = len(shape)
    return pl.BlockSpec(shape, lambda *_: (0,) * nd, pipeline_mode=pl.Buffered(1))


def _inproj(h2d, g, wr, wg, wab):
    m = h2d.shape[0]
    tm = min(ROW_TILE, m)
    row = lambda n: pl.BlockSpec((tm, n), lambda i: (i, 0))
    return pl.pallas_call(
        _inproj_kernel,
        grid=(m // tm,),
        in_specs=[row(D_MODEL), _const_spec((1, D_MODEL)), _const_spec(wr.shape),
                  _const_spec(wg.shape), _const_spec(wab.shape)],
        out_specs=[row(RWKV_COLS), row(GDN_MAIN_COLS), row(AB_COLS)],
        out_shape=[jax.ShapeDtypeStruct((m, RWKV_COLS), F32),
                   jax.ShapeDtypeStruct((m, GDN_MAIN_COLS), F32),
                   jax.ShapeDtypeStruct((m, AB_COLS), F32)],
        compiler_params=pltpu.CompilerParams(
            dimension_semantics=("parallel",), vmem_limit_bytes=VMEM_LIMIT_BYTES),
        name="inproj",
    )(h2d, g, wr, wg, wab)


def _rwkv_prep_kernel(has_vres, *refs):
    if has_vres:
        (p_ref, pp_ref, mu_ref, wup_ref, w0_ref, aup_ref, a0_ref, gup_ref, kk_ref, ka_ref,
         vf_ref, vd_ref, vu_ref, vb_ref, r_o, lw_o, k_o, v_o, kk_o, a_o, g_o) = refs
    else:
        (p_ref, pp_ref, mu_ref, wup_ref, w0_ref, aup_ref, a0_ref, gup_ref, kk_ref, ka_ref,
         r_o, lw_o, k_o, v_o, kk_o, a_o, g_o) = refs
    t = pl.program_id(1)
    p = p_ref[0]
    prev = jnp.where(t == 0, 0.0, pp_ref[0, SUBLANES - 1:SUBLANES, :])
    row = lax.broadcasted_iota(jnp.int32, p.shape, 0)
    shifted = jnp.where(row == 0, prev, pltpu.roll(p, 1, 0))
    p = p + (shifted - p) * mu_ref[...]
    w = RWKV_WIDTH
    r, k, v = p[:, :w], p[:, w:2 * w], p[:, 2 * w:3 * w]
    o = 3 * w
    wd = p[:, o:o + DECAY_RANK]
    ad = p[:, o + DECAY_RANK:o + DECAY_RANK + ICLR_RANK]
    gd = p[:, o + DECAY_RANK + ICLR_RANK:]
    w_raw = -jax.nn.softplus(-(w0_ref[...] + _mm(jnp.tanh(wd), wup_ref[...]))) - 0.5
    a = jax.nn.sigmoid(a0_ref[...] + _mm(ad, aup_ref[...]))
    if has_vres:
        mix = jax.nn.sigmoid(vb_ref[...] + _mm(_mm(v, vd_ref[...]), vu_ref[...]))
        v = v + (vf_ref[0] - v) * mix
    r_o[0] = r
    lw_o[0] = -jnp.exp(w_raw)
    k_o[0] = k * (1.0 + (a - 1.0) * ka_ref[...])
    v_o[0] = v
    kk_o[0] = k * kk_ref[...]
    a_o[0] = a
    g_o[0] = _mm(jax.nn.sigmoid(gd), gup_ref[...])


def _rwkv_prep(p_r, mu, w_up, w0, a_up, a0, g_up, k_k, k_a, vres):
    b, t, _ = p_r.shape
    tt = min(SEQ_TILE, t)
    has_vres = vres is not None
    seq = lambda n: pl.BlockSpec((1, tt, n), lambda i, j: (i, j, 0))
    prev = pl.BlockSpec((1, SUBLANES, RWKV_COLS),
                        lambda i, j: (i, jnp.maximum(j * (tt // SUBLANES) - 1, 0), 0))
    ins = [p_r, p_r, mu, w_up, w0, a_up, a0, g_up, k_k, k_a]
    specs = [seq(RWKV_COLS), prev] + [_const_spec(x.shape) for x in ins[2:]]
    if has_vres:
        v_first, v_down, v_upw, v_bias = vres
        ins += [v_first, v_down, v_upw, v_bias]
        specs += [seq(RWKV_WIDTH)] + [_const_spec(x.shape) for x in (v_down, v_upw, v_bias)]
    out = jax.ShapeDtypeStruct((b, t, RWKV_WIDTH), F32)
    return pl.pallas_call(
        functools.partial(_rwkv_prep_kernel, has_vres),
        grid=(b, t // tt),
        in_specs=specs,
        out_specs=[seq(RWKV_WIDTH)] * 7,
        out_shape=[out] * 7,
        compiler_params=pltpu.CompilerParams(
            dimension_semantics=("parallel", "parallel"), vmem_limit_bytes=VMEM_LIMIT_BYTES),
        name="rwkv_prep",
    )(*ins)


def _wkv_head(r, lw, k, v, kkraw, a, rk, lng, lnb, h0, row, col):
    tt, n = r.shape
    nc = tt // CHUNK
    c3 = lambda x: x.reshape(nc, CHUNK, n)
    kk = kkraw * lax.rsqrt(jnp.sum(kkraw * kkraw, axis=-1, keepdims=True) + L2_EPS)
    bvec = kk * a
    r3, lw3, k3, v3, kk3, b3 = (c3(x) for x in (r, lw, k, v, kk, bvec))

    incl = row >= col
    strict = row > col
    tril = jnp.broadcast_to(incl.astype(BF16), (nc, CHUNK, CHUNK))
    g = _mm_exact_lhs(tril, lw3, BNN)
    g_last = g[:, CHUNK - 1:CHUNK, :]
    e_g = jnp.exp(g)
    e_ng = jnp.exp(-g)
    e_tail = jnp.exp(g_last - g)
    rt = r3 * e_g
    kt = kk3 * jnp.exp(g - lw3)
    kh = k3 * e_ng
    bh = b3 * e_ng
    kbar = k3 * e_tail
    bbar = b3 * e_tail

    a_kb = jnp.where(strict, _mm3(kt, bh, BNT), 0.0)
    a_kk = jnp.where(strict, _mm(kt, kh, BNT), 0.0)
    a_rk = jnp.where(incl, _mm(rt, kh, BNT), 0.0)
    a_rb = jnp.where(incl, _mm(rt, bh, BNT), 0.0)

    tinv = _unit_lower_inverse(a_kb, row, col)
    wmat = _mm3(tinv, kt, BNN)
    uv = _mm3(tinv, _mm(a_kk, v3, BNN), BNN)
    qp = rt - _mm(a_rb, wmat, BNN)
    yi = _mm(a_rk, v3, BNN) - _mm(a_rb, uv, BNN)
    eye = (row == col).astype(F32)
    mmat = eye * jnp.exp(g_last) - _mm3(bbar, wmat, BTN)
    dmat = _mm(kbar, v3, BTN) - _mm(bbar, uv, BTN)

    ys = []
    h = h0
    for c in range(nc):
        ys.append(_mm(qp[c], h) + yi[c])
        h = _mm3(mmat[c], h) + dmat[c]
    y = jnp.concatenate(ys, axis=0)

    mean = jnp.mean(y, axis=-1, keepdims=True)
    var = jnp.mean(jnp.square(y - mean), axis=-1, keepdims=True)
    y = (y - mean) * lax.rsqrt(var + GN_EPS) * lng + lnb
    bonus = jnp.sum(r * k * rk, axis=-1, keepdims=True) * v
    return y + bonus, h


def _wkv_kernel(r_ref, lw_ref, k_ref, v_ref, kk_ref, a_ref, gate_ref, rk_ref, lng_ref, lnb_ref,
                o_ref, h_ref):
    @pl.when(pl.program_id(2) == 0)
    def _():
        h_ref[...] = jnp.zeros_like(h_ref)

    row, col = _tri_masks(CHUNK)
    outs = []
    for hh in range(LANES // RWKV_HEAD):
        sl = slice(hh * RWKV_HEAD, (hh + 1) * RWKV_HEAD)
        y, h = _wkv_head(r_ref[0][:, sl], lw_ref[0][:, sl], k_ref[0][:, sl], v_ref[0][:, sl],
                         kk_ref[0][:, sl], a_ref[0][:, sl], rk_ref[:, sl], lng_ref[:, sl],
                         lnb_ref[:, sl], h_ref[hh], row, col)
        h_ref[hh] = h
        outs.append(y)
    o_ref[0] = jnp.concatenate(outs, axis=-1) * gate_ref[0]


def _wkv(r, lw, k, v, kk, a, gate, r_k, ln_g, ln_b):
    b, t, w = r.shape
    tt = min(SEQ_TILE, t)
    seq = pl.BlockSpec((1, tt, LANES), lambda i, p, j: (i, j, p))
    par = pl.BlockSpec((1, LANES), lambda i, p, j: (0, p))
    return pl.pallas_call(
        _wkv_kernel,
        grid=(b, w // LANES, t // tt),
        in_specs=[seq] * 7 + [par] * 3,
        out_specs=seq,
        out_shape=jax.ShapeDtypeStruct((b, t, w), F32),
        scratch_shapes=[pltpu.VMEM((LANES // RWKV_HEAD, RWKV_HEAD, RWKV_HEAD), F32)],
        compiler_params=pltpu.CompilerParams(
            dimension_semantics=("parallel", "parallel", "arbitrary"),
            vmem_limit_bytes=VMEM_LIMIT_BYTES),
        name="wkv7",
    )(r, lw, k, v, kk, a, gate, r_k, ln_g, ln_b)


def _conv_silu(x, prev8, w, first):
    tt = x.shape[0]
    prev8 = jnp.where(first, 0.0, prev8)
    sub = lax.broadcasted_iota(jnp.int32, prev8.shape, 0)
    acc = x * w[CONV_WIDTH - 1:CONV_WIDTH, :]
    for d in range(1, CONV_WIDTH):
        xr = pltpu.roll(x, d, 0)
        head = jnp.where(sub < d, pltpu.roll(prev8, d, 0), xr[:SUBLANES])
        xs = jnp.concatenate([head, xr[SUBLANES:]], axis=0) if tt > SUBLANES else head
        acc = acc + xs * w[CONV_WIDTH - 1 - d:CONV_WIDTH - d, :]
    return jax.nn.silu(acc)


def _pick_lane(x, idx):
    lane = lax.broadcasted_iota(jnp.int32, x.shape, x.ndim - 1)
    return jnp.sum(jnp.where(lane == idx, x, 0.0), axis=-1, keepdims=True)


def _gdn_kernel(q_ref, k_ref, v_ref, z_ref, qp_ref, kp_ref, vp_ref, cq_ref, ck_ref, cv_ref,
                ab_ref, alog_ref, dtb_ref, ng_ref, o_ref, s_ref):
    hd = pl.program_id(1)
    first = pl.program_id(2) == 0

    @pl.when(first)
    def _():
        s_ref[...] = jnp.zeros_like(s_ref)

    q = _conv_silu(q_ref[0], qp_ref[0], cq_ref[...], first)
    k = _conv_silu(k_ref[0], kp_ref[0], ck_ref[...], first)
    v = _conv_silu(v_ref[0], vp_ref[0], cv_ref[...], first)
    l2 = lambda x: x * lax.rsqrt(jnp.sum(x * x, axis=-1, keepdims=True) + L2_EPS)
    q = l2(q) * (GDN_HEAD ** -0.5)
    k = l2(k)

    ab = ab_ref[0]
    alpha = _pick_lane(ab, hd)
    beta = jax.nn.sigmoid(_pick_lane(ab, hd + GDN_HEADS))
    a_log = _pick_lane(alog_ref[...], hd)
    dt_bias = _pick_lane(dtb_ref[...], hd)
    glog = -jnp.exp(a_log) * jax.nn.softplus(alpha + dt_bias)

    tt, dk = q.shape
    nc = tt // CHUNK
    c3 = lambda x: x.reshape(nc, CHUNK, x.shape[-1])
    q3, k3, v3, g3, beta3 = (c3(x) for x in (q, k, v, glog, beta))

    row, col = _tri_masks(CHUNK)
    incl = row >= col
    strict = row > col
    tril = jnp.broadcast_to(incl.astype(BF16), (nc, CHUNK, CHUNK))
    dm = _mm_exact_lhs(tril, jnp.where(strict, g3, 0.0), BNN)
    gc = dm[:, :, 0:1] + g3[:, 0:1, :]
    g_last = gc[:, CHUNK - 1:CHUNK, :]
    decay = jnp.where(incl, jnp.exp(dm), 0.0)

    kb = k3 * beta3
    vb = v3 * beta3
    low = jnp.where(strict, _mm3(kb, k3, BNT) * decay, 0.0)
    tinv = _unit_lower_inverse(low, row, col)
    u = _mm3(tinv, vb, BNN)
    w_cum = _mm3(tinv, kb * jnp.exp(gc), BNN)
    a_qk = _mm(q3, k3, BNT) * decay
    kd = k3 * jnp.exp(g_last - gc)

    qp = q3 * jnp.exp(gc) - _mm(a_qk, w_cum, BNN)
    yi = _mm(a_qk, u, BNN)
    rowk = lax.broadcasted_iota(jnp.int32, (dk, dk), 0)
    colk = lax.broadcasted_iota(jnp.int32, (dk, dk), 1)
    eye = (rowk == colk).astype(F32)
    mmat = eye * jnp.exp(g_last) - _mm3(kd, w_cum, BTN)
    dmat = _mm(kd, u, BTN)

    s = s_ref[...]
    outs = []
    for c in range(nc):
        outs.append(_mm(qp[c], s) + yi[c])
        s = _mm3(mmat[c], s) + dmat[c]
    s_ref[...] = s
    o = jnp.concatenate(outs, axis=0)
    o = o * lax.rsqrt(jnp.mean(o * o, axis=-1, keepdims=True) + GATED_NORM_EPS)
    o_ref[0] = o * ng_ref[...] * jax.nn.silu(z_ref[0])


def _gdn(p_g, p_ab, conv_w, a_log8, dt_bias8, norm_g):
    b, t, _ = p_g.shape
    tt = min(SEQ_TILE, t)
    nh = GDN_HEADS
    seq = lambda off: pl.BlockSpec((1, tt, LANES), lambda i, h, j: (i, j, off + h))
    prev = lambda off: pl.BlockSpec(
        (1, SUBLANES, LANES),
        lambda i, h, j: (i, jnp.maximum(j * (tt // SUBLANES) - 1, 0), off + h))
    cw = lambda off: pl.BlockSpec((CONV_WIDTH, LANES), lambda i, h, j: (0, off + h))
    small = lambda n: pl.BlockSpec((1, n), lambda i, h, j: (0, 0))
    return pl.pallas_call(
        _gdn_kernel,
        grid=(b, nh, t // tt),
        in_specs=[seq(0), seq(nh), seq(2 * nh), seq(3 * nh), prev(0), prev(nh), prev(2 * nh),
                  cw(0), cw(nh), cw(2 * nh),
                  pl.BlockSpec((1, tt, AB_COLS), lambda i, h, j: (i, j, 0)),
                  small(AB_COLS), small(AB_COLS), small(LANES)],
        out_specs=pl.BlockSpec((1, tt, LANES), lambda i, h, j: (i, j, h)),
        out_shape=jax.ShapeDtypeStruct((b, t, GDN_WIDTH), F32),
        scratch_shapes=[pltpu.VMEM((GDN_HEAD, GDN_HEAD), F32)],
        compiler_params=pltpu.CompilerParams(
            dimension_semantics=("parallel", "parallel", "arbitrary"),
            vmem_limit_bytes=VMEM_LIMIT_BYTES),
        name="gdn",
    )(p_g, p_g, p_g, p_g, p_g, p_g, p_g, conv_w, conv_w, conv_w, p_ab, a_log8, dt_bias8, norm_g)


def _out_mlp_kernel(final, ya_ref, yb_ref, h_ref, wo_ref, g2_ref, up_ref, dn_ref, gf_ref, o_ref):
    w = RWKV_WIDTH
    h = h_ref[...]
    h = h + jnp.dot(ya_ref[...].astype(BF16), wo_ref[:w, :], preferred_element_type=F32)
    h = h + jnp.dot(yb_ref[...].astype(BF16), wo_ref[w:, :], preferred_element_type=F32)
    xb = _rmsnorm(h, g2_ref[...], NORM_EPS).astype(BF16)
    hid = jnp.dot(xb, up_ref[...], preferred_element_type=F32)
    hid = jnp.square(jnp.maximum(hid, 0.0)).astype(BF16)
    acc = h + jnp.dot(hid, dn_ref[...], preferred_element_type=F32)
    if final:
        acc = _rmsnorm(acc, gf_ref[...], NORM_EPS)
    o_ref[...] = acc


def _out_mlp(ya, yb, h2d, w_out, g2, up, dn, gf, final):
    m = h2d.shape[0]
    tm = min(ROW_TILE, m)
    row = lambda n: pl.BlockSpec((tm, n), lambda i: (i, 0))
    return pl.pallas_call(
        functools.partial(_out_mlp_kernel, final),
        grid=(m // tm,),
        in_specs=[row(RWKV_WIDTH), row(GDN_WIDTH), row(D_MODEL), _const_spec(w_out.shape),
                  _const_spec(g2.shape), _const_spec(up.shape), _const_spec(dn.shape),
                  _const_spec(gf.shape)],
        out_specs=row(D_MODEL),
        out_shape=jax.ShapeDtypeStruct((m, D_MODEL), F32),
        compiler_params=pltpu.CompilerParams(
            dimension_semantics=("parallel",), vmem_limit_bytes=VMEM_LIMIT_BYTES),
        name="out_mlp",
    )(ya, yb, h2d, w_out, g2, up, dn, gf)


def kernel(x, norm1_g, w_in, shift_mu, rw_w_up, rw_w0, rw_a_up, rw_a0, rw_g_up, rw_k_k, rw_k_a,
           rw_r_k, rw_ln_g, rw_ln_b, rw_vres_down, rw_vres_up, rw_vres_b, gdn_conv, gdn_A_log,
           gdn_dt_bias, gdn_norm_g, w_out, norm2_g, mlp_up, mlp_down, final_g):
    b, t, d = x.shape
    depth = w_in.shape[0]
    m = b * t
    row = lambda v: v.reshape(1, -1)
    pad8 = lambda v: jnp.pad(v, (0, AB_COLS - v.shape[0])).reshape(1, AB_COLS)

    h = x.reshape(m, d)
    v_first = None
    for l in range(depth):
        wl = w_in[l].astype(BF16)
        wr = wl[:, :RWKV_COLS]
        wg = wl[:, RWKV_COLS:RWKV_COLS + GDN_MAIN_COLS]
        wab = jnp.pad(wl[:, RWKV_COLS + GDN_MAIN_COLS:], ((0, 0), (0, LANES - AB_COLS)))
        p_r, p_g, p_ab = _inproj(h, row(norm1_g[l]), wr, wg, wab)
        p_r = p_r.reshape(b, t, RWKV_COLS)
        p_g = p_g.reshape(b, t, GDN_MAIN_COLS)
        p_ab = p_ab.reshape(b, t, AB_COLS)

        vres = None
        if l > 0:
            vres = (v_first, rw_vres_down[l - 1].astype(BF16), rw_vres_up[l - 1].astype(BF16),
                    row(rw_vres_b[l - 1]))
        r, lw, k, v, kk, a, gate = _rwkv_prep(
            p_r, row(shift_mu[l]), rw_w_up[l].astype(BF16), row(rw_w0[l]),
            rw_a_up[l].astype(BF16), row(rw_a0[l]), rw_g_up[l].astype(BF16),
            row(rw_k_k[l]), row(rw_k_a[l]), vres)
        if l == 0:
            v_first = v
        y_a = _wkv(r, lw, k, v, kk, a, gate, row(rw_r_k[l]), row(rw_ln_g[l]), row(rw_ln_b[l]))
        y_b = _gdn(p_g, p_ab, gdn_conv[l], pad8(gdn_A_log[l]), pad8(gdn_dt_bias[l]),
                   row(gdn_norm_g[l]))
        h = _out_mlp(y_a.reshape(m, RWKV_WIDTH), y_b.reshape(m, GDN_WIDTH), h,
                     w_out[l].astype(BF16), row(norm2_g[l]), mlp_up[l].astype(BF16),
                     mlp_down[l].astype(BF16), row(final_g), l == depth - 1)
    return h.reshape(b, t, d)
```

```python
import functools

import jax
import jax.numpy as jnp
from jax import lax
from jax.experimental import pallas as pl
from jax.experimental.pallas import tpu as pltpu

F32 = jnp.float32
BF16 = jnp.bfloat16

D_MODEL = 1024
RWKV_WIDTH = 512
RWKV_HEAD = 64
DECAY_RANK = 64
ICLR_RANK = 64
GATE_RANK = 128
GN_EPS = 64e-5
GDN_WIDTH = 512
GDN_HEAD = 128
GDN_HEADS = 4
CONV_WIDTH = 4
CHUNK = 64
D_FF = 4 * D_MODEL
NORM_EPS = 1e-5
GATED_NORM_EPS = 1e-6
L2_EPS = 1e-6
RWKV_COLS = 3 * RWKV_WIDTH + DECAY_RANK + ICLR_RANK + GATE_RANK
GDN_MAIN_COLS = 4 * GDN_WIDTH
AB_COLS = 2 * GDN_HEADS

LANES = 128
SUBLANES = 8
VMEM_LIMIT_BYTES = 56 * 1024 * 1024

ROW_TILE = 512
SEQ_TILE = 1024

NN = (((1,), (0,)), ((), ()))
NT = (((1,), (1,)), ((), ()))
BNN = (((2,), (1,)), ((0,), (0,)))
BNT = (((2,), (2,)), ((0,), (0,)))
BTN = (((1,), (1,)), ((0,), (0,)))
TN = (((0,), (0,)), ((), ()))


def _mm(a, b, dims=NN):
    return lax.dot_general(a.astype(BF16), b.astype(BF16), dims, preferred_element_type=F32)


def _mm_exact_lhs(a_bf16, b, dims=NN):
    b1 = b.astype(BF16)
    r1 = b - b1.astype(F32)
    b2 = r1.astype(BF16)
    b3 = (r1 - b2.astype(F32)).astype(BF16)
    d = lambda y: lax.dot_general(a_bf16, y, dims, preferred_element_type=F32)
    return d(b1) + (d(b2) + d(b3))


def _rmsnorm(x, g, eps):
    return x * lax.rsqrt(jnp.mean(x * x, axis=-1, keepdims=True) + eps) * g


def _tri_masks(c):
    row = lax.broadcasted_iota(jnp.int32, (c, c), 0)
    col = lax.broadcasted_iota(jnp.int32, (c, c), 1)
    return row, col


def _unit_lower_inverse_minus_eye(low, row, col, block):
    same = lambda s: (row ^ col) < s
    n = -jnp.where(same(2), low, 0.0)
    s = 2
    while s < block:
        e = jnp.where(same(2 * s) & jnp.logical_not(same(s)), low, 0.0)
        x = _mm(n, e, BNN)
        n = n - e - x - _mm(e + x, n, BNN)
        s *= 2
    return n


def _inproj_kernel(x_ref, g_ref, wr_ref, wg_ref, wab_ref, pr_ref, pg_ref, pab_ref):
    xb = _rmsnorm(x_ref[...], g_ref[...], NORM_EPS).astype(BF16)
    pr_ref[...] = jnp.dot(xb, wr_ref[...], preferred_element_type=F32)
    pg_ref[...] = jnp.dot(xb, wg_ref[...], preferred_element_type=F32)
    pab_ref[...] = jnp.dot(xb, wab_ref[...], preferred_element_type=F32)[:, :AB_COLS]


def _const_spec(shape):
    nd = len(shape)
    return pl.BlockSpec(shape, lambda *_: (0,) * nd, pipeline_mode=pl.Buffered(1))


def _inproj(h2d, g, wr, wg, wab):
    m = h2d.shape[0]
    tm = min(ROW_TILE, m)
    row = lambda n: pl.BlockSpec((tm, n), lambda i: (i, 0))
    return pl.pallas_call(
        _inproj_kernel,
        grid=(m // tm,),
        in_specs=[row(D_MODEL), _const_spec((1, D_MODEL)), _const_spec(wr.shape),
                  _const_spec(wg.shape), _const_spec(wab.shape)],
        out_specs=[row(RWKV_COLS), row(GDN_MAIN_COLS), row(AB_COLS)],
        out_shape=[jax.ShapeDtypeStruct((m, RWKV_COLS), F32),
                   jax.ShapeDtypeStruct((m, GDN_MAIN_COLS), F32),
                   jax.ShapeDtypeStruct((m, AB_COLS), F32)],
        compiler_params=pltpu.CompilerParams(
            dimension_semantics=("parallel",), vmem_limit_bytes=VMEM_LIMIT_BYTES),
        name="inproj",
    )(h2d, g, wr, wg, wab)


def _rwkv_prep_kernel(has_vres, *refs):
    if has_vres:
        (p_ref, pp_ref, mu_ref, wup_ref, w0_ref, aup_ref, a0_ref, gup_ref, kk_ref, ka_ref,
         vf_ref, vd_ref, vu_ref, vb_ref, r_o, lw_o, k_o, v_o, kk_o, a_o, g_o) = refs
    else:
        (p_ref, pp_ref, mu_ref, wup_ref, w0_ref, aup_ref, a0_ref, gup_ref, kk_ref, ka_ref,
         r_o, lw_o, k_o, v_o, kk_o, a_o, g_o) = refs
    t = pl.program_id(1)
    p = p_ref[0]
    prev = jnp.where(t == 0, 0.0, pp_ref[0, SUBLANES - 1:SUBLANES, :])
    row = lax.broadcasted_iota(jnp.int32, p.shape, 0)
    shifted = jnp.where(row == 0, prev, pltpu.roll(p, 1, 0))
    p = p + (shifted - p) * mu_ref[...]
    w = RWKV_WIDTH
    r, k, v = p[:, :w], p[:, w:2 * w], p[:, 2 * w:3 * w]
    o = 3 * w
    wd = p[:, o:o + DECAY_RANK]
    ad = p[:, o + DECAY_RANK:o + DECAY_RANK + ICLR_RANK]
    gd = p[:, o + DECAY_RANK + ICLR_RANK:]
    w_raw = -jax.nn.softplus(-(w0_ref[...] + _mm(jnp.tanh(wd), wup_ref[...]))) - 0.5
    a = jax.nn.sigmoid(a0_ref[...] + _mm(ad, aup_ref[...]))
    if has_vres:
        mix = jax.nn.sigmoid(vb_ref[...] + _mm(_mm(v, vd_ref[...]), vu_ref[...]))
        v = v + (vf_ref[0] - v) * mix
    r_o[0] = r
    lw_o[0] = -jnp.exp(w_raw)
    k_o[0] = k * (1.0 + (a - 1.0) * ka_ref[...])
    v_o[0] = v
    kk_o[0] = k * kk_ref[...]
    a_o[0] = a
    g_o[0] = _mm(jax.nn.sigmoid(gd), gup_ref[...])


def _rwkv_prep(p_r, mu, w_up, w0, a_up, a0, g_up, k_k, k_a, vres):
    b, t, _ = p_r.shape
    tt = min(ROW_TILE, t)
    has_vres = vres is not None
    seq = lambda n: pl.BlockSpec((1, tt, n), lambda i, j: (i, j, 0))
    prev = pl.BlockSpec((1, SUBLANES, RWKV_COLS),
                        lambda i, j: (i, jnp.maximum(j * (tt // SUBLANES) - 1, 0), 0))
    ins = [p_r, p_r, mu, w_up, w0, a_up, a0, g_up, k_k, k_a]
    specs = [seq(RWKV_COLS), prev] + [_const_spec(x.shape) for x in ins[2:]]
    if has_vres:
        v_first, v_down, v_upw, v_bias = vres
        ins += [v_first, v_down, v_upw, v_bias]
        specs += [seq(RWKV_WIDTH)] + [_const_spec(x.shape) for x in (v_down, v_upw, v_bias)]
    out = jax.ShapeDtypeStruct((b, t, RWKV_WIDTH), F32)
    return pl.pallas_call(
        functools.partial(_rwkv_prep_kernel, has_vres),
        grid=(b, t // tt),
        in_specs=specs,
        out_specs=[seq(RWKV_WIDTH)] * 7,
        out_shape=[out] * 7,
        compiler_params=pltpu.CompilerParams(
            dimension_semantics=("parallel", "parallel"), vmem_limit_bytes=VMEM_LIMIT_BYTES),
        name="rwkv_prep",
    )(*ins)


def _stack_heads(x3):
    lo = lax.broadcasted_iota(jnp.int32, x3.shape, 2) < RWKV_HEAD
    return jnp.concatenate([jnp.where(lo, x3, 0.0), jnp.where(lo, 0.0, x3)], axis=1)


def _head_sum(x):
    lo = lax.broadcasted_iota(jnp.int32, x.shape, 1) < RWKV_HEAD
    s0 = jnp.sum(jnp.where(lo, x, 0.0), axis=-1, keepdims=True)
    s1 = jnp.sum(jnp.where(lo, 0.0, x), axis=-1, keepdims=True)
    return jnp.where(lo, s0, s1)


def _wkv_kernel(r_ref, lw_ref, k_ref, v_ref, kk_ref, a_ref, gate_ref, rk_ref, lng_ref, lnb_ref,
                o_ref, h_ref):
    @pl.when(pl.program_id(2) == 0)
    def _():
        h_ref[...] = jnp.zeros_like(h_ref)

    r, lw, k, v, kkraw, a = (x[0] for x in (r_ref, lw_ref, k_ref, v_ref, kk_ref, a_ref))
    tt = r.shape[0]
    nc = tt // CHUNK
    pp = 2 * CHUNK
    kk = kkraw * lax.rsqrt(_head_sum(kkraw * kkraw) + L2_EPS)
    c3 = lambda x: x.reshape(nc, CHUNK, LANES)
    r3, lw3, k3, v3, kk3, b3 = (c3(x) for x in (r, lw, k, v, kk, kk * a))

    row_c, col_c = _tri_masks(CHUNK)
    tril = jnp.broadcast_to((row_c >= col_c).astype(BF16), (nc, CHUNK, CHUNK))
    g = _mm_exact_lhs(tril, lw3, BNN)
    g_last = g[:, CHUNK - 1:CHUNK, :]
    e_ng = jnp.exp(-g)
    e_tail = jnp.exp(g_last - g)
    rt = _stack_heads(r3 * jnp.exp(g))
    kt = _stack_heads(kk3 * jnp.exp(g - lw3))
    kh = _stack_heads(k3 * e_ng)
    bh = _stack_heads(b3 * e_ng)
    kbar = _stack_heads(k3 * e_tail)
    bbar = _stack_heads(b3 * e_tail)
    vs = _stack_heads(v3)

    row, col = _tri_masks(pp)
    same = (row ^ col) < CHUNK
    strict = (row > col) & same
    incl = (row >= col) & same
    a_kb = jnp.where(strict, _mm(kt, bh, BNT), 0.0)
    a_kk = jnp.where(strict, _mm(kt, kh, BNT), 0.0)
    a_rk = jnp.where(incl, _mm(rt, kh, BNT), 0.0)
    a_rb = jnp.where(incl, _mm(rt, bh, BNT), 0.0)

    n = _unit_lower_inverse_minus_eye(a_kb, row, col, CHUNK)
    rhs = jnp.concatenate([kt, _mm(a_kk, vs, BNN)], axis=2)
    wu = rhs + _mm(n, rhs, BNN)
    corr = _mm(a_rb, wu, BNN)
    qp = rt - corr[:, :, :pp]
    yi = _mm(a_rk, vs, BNN) - corr[:, :, pp:]
    bw = _mm(bbar, wu, BTN)
    mc = bw[:, :, :pp]
    dmat = _mm(kbar, vs, BTN) - bw[:, :, pp:]
    gcol = jnp.sum(jnp.where(row == col, jnp.exp(g_last), 0.0), axis=-1, keepdims=True)

    h = h_ref[...]
    ys = []
    for c in range(nc):
        z = _mm(jnp.concatenate([qp[c], mc[c]], axis=0), h)
        yc = z[:pp] + yi[c]
        ys.append(yc[:CHUNK] + yc[CHUNK:])
        h = gcol[c] * h - z[pp:] + dmat[c]
    h_ref[...] = h
    y = jnp.concatenate(ys, axis=0)

    inv_n = 1.0 / RWKV_HEAD
    d = y - _head_sum(y) * inv_n
    y = d * lax.rsqrt(_head_sum(d * d) * inv_n + GN_EPS) * lng_ref[...] + lnb_ref[...]
    bonus = _head_sum(r * k * rk_ref[...]) * v
    o_ref[0] = (y + bonus) * gate_ref[0]


def _wkv(r, lw, k, v, kk, a, gate, r_k, ln_g, ln_b):
    b, t, w = r.shape
    tt = min(SEQ_TILE, t)
    seq = pl.BlockSpec((1, tt, LANES), lambda i, p, j: (i, j, p))
    par = pl.BlockSpec((1, LANES), lambda i, p, j: (0, p))
    return pl.pallas_call(
        _wkv_kernel,
        grid=(b, w // LANES, t // tt),
        in_specs=[seq] * 7 + [par] * 3,
        out_specs=seq,
        out_shape=jax.ShapeDtypeStruct((b, t, w), F32),
        scratch_shapes=[pltpu.VMEM((LANES, LANES), F32)],
        compiler_params=pltpu.CompilerParams(
            dimension_semantics=("parallel", "parallel", "arbitrary"),
            vmem_limit_bytes=VMEM_LIMIT_BYTES),
        name="wkv7",
    )(r, lw, k, v, kk, a, gate, r_k, ln_g, ln_b)


def _conv_silu(x, prev8, w, first):
    tt = x.shape[0]
    prev8 = jnp.where(first, 0.0, prev8)
    sub = lax.broadcasted_iota(jnp.int32, prev8.shape, 0)
    acc = x * w[CONV_WIDTH - 1:CONV_WIDTH, :]
    for d in range(1, CONV_WIDTH):
        xr = pltpu.roll(x, d, 0)
        head = jnp.where(sub < d, pltpu.roll(prev8, d, 0), xr[:SUBLANES])
        xs = jnp.concatenate([head, xr[SUBLANES:]], axis=0) if tt > SUBLANES else head
        acc = acc + xs * w[CONV_WIDTH - 1 - d:CONV_WIDTH - d, :]
    return jax.nn.silu(acc)


def _pick_lane(x, idx):
    lane = lax.broadcasted_iota(jnp.int32, x.shape, x.ndim - 1)
    return jnp.sum(jnp.where(lane == idx, x, 0.0), axis=-1, keepdims=True)


def _gdn_kernel(q_ref, k_ref, v_ref, z_ref, qp_ref, kp_ref, vp_ref, cq_ref, ck_ref, cv_ref,
                ab_ref, alog_ref, dtb_ref, ng_ref, o_ref, s_ref):
    hd = pl.program_id(1)
    first = pl.program_id(2) == 0

    @pl.when(first)
    def _():
        s_ref[...] = jnp.zeros_like(s_ref)

    q = _conv_silu(q_ref[0], qp_ref[0], cq_ref[...], first)
    k = _conv_silu(k_ref[0], kp_ref[0], ck_ref[...], first)
    v = _conv_silu(v_ref[0], vp_ref[0], cv_ref[...], first)
    l2 = lambda x: x * lax.rsqrt(jnp.sum(x * x, axis=-1, keepdims=True) + L2_EPS)
    q = l2(q) * (GDN_HEAD ** -0.5)
    k = l2(k)

    ab = ab_ref[0]
    alpha = _pick_lane(ab, hd)
    beta = jax.nn.sigmoid(_pick_lane(ab, hd + GDN_HEADS))
    a_log = _pick_lane(alog_ref[...], hd)
    dt_bias = _pick_lane(dtb_ref[...], hd)
    glog = -jnp.exp(a_log) * jax.nn.softplus(alpha + dt_bias)

    tt, dk = q.shape
    pp = 2 * CHUNK
    nv = tt // pp
    c3 = lambda x: x.reshape(nv, pp, x.shape[-1])
    q3, k3, v3, g3, beta3 = (c3(x) for x in (q, k, v, glog, beta))

    row, col = _tri_masks(pp)
    same = (row ^ col) < CHUNK
    strict = (row > col) & same
    incl = (row >= col) & same
    tril = jnp.broadcast_to(incl.astype(BF16), (nv, pp, pp))
    dm = _mm_exact_lhs(tril, jnp.where(strict, g3, 0.0), BNN)
    top = lax.broadcasted_iota(jnp.int32, (pp, 1), 0) < CHUNK
    gc = jnp.where(top, dm[:, :, 0:1] + g3[:, 0:1, :],
                   dm[:, :, CHUNK:CHUNK + 1] + g3[:, CHUNK:CHUNK + 1, :])
    g_last = jnp.where(top, gc[:, CHUNK - 1:CHUNK, :], gc[:, pp - 1:pp, :])
    decay = jnp.where(incl, jnp.exp(dm), 0.0)

    kb = k3 * beta3
    sc = _mm(jnp.concatenate([kb, q3], axis=1), k3, BNT)
    low = jnp.where(strict, sc[:, :pp] * decay, 0.0)
    a_qk = sc[:, pp:] * decay
    n = _unit_lower_inverse_minus_eye(low, row, col, CHUNK)
    e_gc = jnp.exp(gc)
    rhs = jnp.concatenate([kb * e_gc, v3 * beta3], axis=2)
    wu = rhs + _mm(n, rhs, BNN)
    corr = _mm(a_qk, wu, BNN)
    qp = q3 * e_gc - corr[:, :, :dk]
    yi = corr[:, :, dk:]
    kd = k3 * jnp.exp(g_last - gc)
    e_last = jnp.exp(g_last)

    s = s_ref[...]
    outs = []
    for j in range(nv):
        for half in range(2):
            rows = slice(half * CHUNK, (half + 1) * CHUNK)
            md = _mm(kd[j, rows], wu[j, rows], TN)
            z = _mm(jnp.concatenate([qp[j, rows], md[:, :dk]], axis=0), s)
            outs.append(z[:CHUNK] + yi[j, rows])
            s = e_last[j, half * CHUNK:half * CHUNK + 1, :] * s - z[CHUNK:] + md[:, dk:]
    s_ref[...] = s
    o = jnp.concatenate(outs, axis=0)
    o = o * lax.rsqrt(jnp.mean(o * o, axis=-1, keepdims=True) + GATED_NORM_EPS)
    o_ref[0] = o * ng_ref[...] * jax.nn.silu(z_ref[0])


def _gdn(p_g, p_ab, conv_w, a_log8, dt_bias8, norm_g):
    b, t, _ = p_g.shape
    tt = min(SEQ_TILE, t)
    nh = GDN_HEADS
    seq = lambda off: pl.BlockSpec((1, tt, LANES), lambda i, h, j: (i, j, off + h))
    prev = lambda off: pl.BlockSpec(
        (1, SUBLANES, LANES),
        lambda i, h, j: (i, jnp.maximum(j * (tt // SUBLANES) - 1, 0), off + h))
    cw = lambda off: pl.BlockSpec((CONV_WIDTH, LANES), lambda i, h, j: (0, off + h))
    small = lambda n: pl.BlockSpec((1, n), lambda i, h, j: (0, 0))
    return pl.pallas_call(
        _gdn_kernel,
        grid=(b, nh, t // tt),
        in_specs=[seq(0), seq(nh), seq(2 * nh), seq(3 * nh), prev(0), prev(nh), prev(2 * nh),
                  cw(0), cw(nh), cw(2 * nh),
                  pl.BlockSpec((1, tt, AB_COLS), lambda i, h, j: (i, j, 0)),
                  small(AB_COLS), small(AB_COLS), small(LANES)],
        out_specs=pl.BlockSpec((1, tt, LANES), lambda i, h, j: (i, j, h)),
        out_shape=jax.ShapeDtypeStruct((b, t, GDN_WIDTH), F32),
        scratch_shapes=[pltpu.VMEM((GDN_HEAD, GDN_HEAD), F32)],
        compiler_params=pltpu.CompilerParams(
            dimension_semantics=("parallel", "parallel", "arbitrary"),
            vmem_limit_bytes=VMEM_LIMIT_BYTES),
        name="gdn",
    )(p_g, p_g, p_g, p_g, p_g, p_g, p_g, conv_w, conv_w, conv_w, p_ab, a_log8, dt_bias8, norm_g)


def _out_mlp_kernel(final, ya_ref, yb_ref, h_ref, wo_ref, g2_ref, up_ref, dn_ref, gf_ref, o_ref):
    w = RWKV_WIDTH
    h = h_ref[...]
    h = h + jnp.dot(ya_ref[...].astype(BF16), wo_ref[:w, :], preferred_element_type=F32)
    h = h + jnp.dot(yb_ref[...].astype(BF16), wo_ref[w:, :], preferred_element_type=F32)
    xb = _rmsnorm(h, g2_ref[...], NORM_EPS).astype(BF16)
    hid = jnp.dot(xb, up_ref[...], preferred_element_type=F32)
    hid = jnp.square(jnp.maximum(hid, 0.0)).astype(BF16)
    acc = h + jnp.dot(hid, dn_ref[...], preferred_element_type=F32)
    if final:
        acc = _rmsnorm(acc, gf_ref[...], NORM_EPS)
    o_ref[...] = acc


def _out_mlp(ya, yb, h2d, w_out, g2, up, dn, gf, final):
    m = h2d.shape[0]
    tm = min(ROW_TILE, m)
    row = lambda n: pl.BlockSpec((tm, n), lambda i: (i, 0))
    return pl.pallas_call(
        functools.partial(_out_mlp_kernel, final),
        grid=(m // tm,),
        in_specs=[row(RWKV_WIDTH), row(GDN_WIDTH), row(D_MODEL), _const_spec(w_out.shape),
                  _const_spec(g2.shape), _const_spec(up.shape), _const_spec(dn.shape),
                  _const_spec(gf.shape)],
        out_specs=row(D_MODEL),
        out_shape=jax.ShapeDtypeStruct((m, D_MODEL), F32),
        compiler_params=pltpu.CompilerParams(
            dimension_semantics=("parallel",), vmem_limit_bytes=VMEM_LIMIT_BYTES),
        name="out_mlp",
    )(ya, yb, h2d, w_out, g2, up, dn, gf)


def kernel(x, norm1_g, w_in, shift_mu, rw_w_up, rw_w0, rw_a_up, rw_a0, rw_g_up, rw_k_k, rw_k_a,
           rw_r_k, rw_ln_g, rw_ln_b, rw_vres_down, rw_vres_up, rw_vres_b, gdn_conv, gdn_A_log,
           gdn_dt_bias, gdn_norm_g, w_out, norm2_g, mlp_up, mlp_down, final_g):
    b, t, d = x.shape
    depth = w_in.shape[0]
    m = b * t
    row = lambda v: v.reshape(1, -1)
    pad8 = lambda v: jnp.pad(v, (0, AB_COLS - v.shape[0])).reshape(1, AB_COLS)

    h = x.reshape(m, d)
    v_first = None
    for l in range(depth):
        wl = w_in[l].astype(BF16)
        wr = wl[:, :RWKV_COLS]
        wg = wl[:, RWKV_COLS:RWKV_COLS + GDN_MAIN_COLS]
        wab = jnp.pad(wl[:, RWKV_COLS + GDN_MAIN_COLS:], ((0, 0), (0, LANES - AB_COLS)))
        p_r, p_g, p_ab = _inproj(h, row(norm1_g[l]), wr, wg, wab)
        p_r = p_r.reshape(b, t, RWKV_COLS)
        p_g = p_g.reshape(b, t, GDN_MAIN_COLS)
        p_ab = p_ab.reshape(b, t, AB_COLS)

        vres = None
        if l > 0:
            vres = (v_first, rw_vres_down[l - 1].astype(BF16), rw_vres_up[l - 1].astype(BF16),
                    row(rw_vres_b[l - 1]))
        r, lw, k, v, kk, a, gate = _rwkv_prep(
            p_r, row(shift_mu[l]), rw_w_up[l].astype(BF16), row(rw_w0[l]),
            rw_a_up[l].astype(BF16), row(rw_a0[l]), rw_g_up[l].astype(BF16),
            row(rw_k_k[l]), row(rw_k_a[l]), vres)
        if l == 0:
            v_first = v
        y_a = _wkv(r, lw, k, v, kk, a, gate, row(rw_r_k[l]), row(rw_ln_g[l]), row(rw_ln_b[l]))
        y_b = _gdn(p_g, p_ab, gdn_conv[l], pad8(gdn_A_log[l]), pad8(gdn_dt_bias[l]),
                   row(gdn_norm_g[l]))
        h = _out_mlp(y_a.reshape(m, RWKV_WIDTH), y_b.reshape(m, GDN_WIDTH), h,
                     w_out[l].astype(BF16), row(norm2_g[l]), mlp_up[l].astype(BF16),
                     mlp_down[l].astype(BF16), row(final_g), l == depth - 1)
    return h.reshape(b, t, d)
```

```python
import functools
import math

import jax
import jax.numpy as jnp
from jax import lax
from jax.experimental import pallas as pl
from jax.experimental.pallas import tpu as pltpu

F32 = jnp.float32
BF16 = jnp.bfloat16

D_MODEL = 1024
RWKV_WIDTH = 512
RWKV_HEAD = 64
DECAY_RANK = 64
ICLR_RANK = 64
GATE_RANK = 128
GN_EPS = 64e-5
GDN_WIDTH = 512
GDN_HEAD = 128
GDN_HEADS = 4
CONV_WIDTH = 4
CHUNK = 64
D_FF = 4 * D_MODEL
NORM_EPS = 1e-5
GATED_NORM_EPS = 1e-6
L2_EPS = 1e-6
RWKV_COLS = 3 * RWKV_WIDTH + DECAY_RANK + ICLR_RANK + GATE_RANK
GDN_MAIN_COLS = 4 * GDN_WIDTH
AB_COLS = 2 * GDN_HEADS

LANES = 128
SUBLANES = 8
VMEM_LIMIT_BYTES = 56 * 1024 * 1024

ROW_TILE = 512
SEQ_TILE = 256
SEQ_BATCH = 8

NN = (((1,), (0,)), ((), ()))
NT = (((1,), (1,)), ((), ()))
BNN = (((2,), (1,)), ((0,), (0,)))
BNT = (((2,), (2,)), ((0,), (0,)))
BTN = (((1,), (1,)), ((0,), (0,)))
TN = (((0,), (0,)), ((), ()))


def _mm(a, b, dims=NN):
    return lax.dot_general(a.astype(BF16), b.astype(BF16), dims, preferred_element_type=F32)


def _mm_exact_lhs(a_bf16, b, dims=NN):
    b1 = b.astype(BF16)
    r1 = b - b1.astype(F32)
    b2 = r1.astype(BF16)
    b3 = (r1 - b2.astype(F32)).astype(BF16)
    d = lambda y: lax.dot_general(a_bf16, y, dims, preferred_element_type=F32)
    return d(b1) + (d(b2) + d(b3))


def _mm_exact_rhs(a, b_bf16, dims=NN):
    a1 = a.astype(BF16)
    r1 = a - a1.astype(F32)
    a2 = r1.astype(BF16)
    a3 = (r1 - a2.astype(F32)).astype(BF16)
    d = lambda y: lax.dot_general(y, b_bf16, dims, preferred_element_type=F32)
    return d(a1) + (d(a2) + d(a3))


def _rmsnorm(x, g, eps):
    return x * lax.rsqrt(jnp.mean(x * x, axis=-1, keepdims=True) + eps) * g


def _tri_masks(c):
    row = lax.broadcasted_iota(jnp.int32, (c, c), 0)
    col = lax.broadcasted_iota(jnp.int32, (c, c), 1)
    return row, col


def _unit_lower_inverse_minus_eye(low, row, col, block):
    same = lambda s: (row ^ col) < s
    n = -jnp.where(same(2), low, 0.0)
    s = 2
    while s < block:
        e = jnp.where(same(2 * s) & jnp.logical_not(same(s)), low, 0.0)
        x = _mm(n, e, BNN)
        n = n - e - x - _mm(e + x, n, BNN)
        s *= 2
    return n


def _inproj_kernel(x_ref, g_ref, wr_ref, wg_ref, wab_ref, pr_ref, pg_ref, pab_ref):
    xb = _rmsnorm(x_ref[...], g_ref[...], NORM_EPS).astype(BF16)
    pr_ref[...] = jnp.dot(xb, wr_ref[...], preferred_element_type=F32)
    pg_ref[...] = jnp.dot(xb, wg_ref[...], preferred_element_type=F32)
    pab_ref[...] = jnp.dot(xb, wab_ref[...], preferred_element_type=F32)[:, :AB_COLS]


def _const_spec(shape):
    nd = len(shape)
    return pl.BlockSpec(shape, lambda *_: (0,) * nd, pipeline_mode=pl.Buffered(1))


def _inproj(h2d, g, wr, wg, wab):
    m = h2d.shape[0]
    tm = min(ROW_TILE, m)
    row = lambda n: pl.BlockSpec((tm, n), lambda i: (i, 0))
    return pl.pallas_call(
        _inproj_kernel,
        grid=(m // tm,),
        in_specs=[row(D_MODEL), _const_spec((1, D_MODEL)), _const_spec(wr.shape),
                  _const_spec(wg.shape), _const_spec(wab.shape)],
        out_specs=[row(RWKV_COLS), row(GDN_MAIN_COLS), row(AB_COLS)],
        out_shape=[jax.ShapeDtypeStruct((m, RWKV_COLS), F32),
                   jax.ShapeDtypeStruct((m, GDN_MAIN_COLS), F32),
                   jax.ShapeDtypeStruct((m, AB_COLS), F32)],
        compiler_params=pltpu.CompilerParams(
            dimension_semantics=("parallel",), vmem_limit_bytes=VMEM_LIMIT_BYTES),
        name="inproj",
    )(h2d, g, wr, wg, wab)


def _rwkv_prep_kernel(has_vres, *refs):
    if has_vres:
        (p_ref, pp_ref, mu_ref, wup_ref, w0_ref, aup_ref, a0_ref, gup_ref, kk_ref, ka_ref,
         vf_ref, vd_ref, vu_ref, vb_ref, r_o, lw_o, k_o, v_o, kk_o, a_o, g_o) = refs
    else:
        (p_ref, pp_ref, mu_ref, wup_ref, w0_ref, aup_ref, a0_ref, gup_ref, kk_ref, ka_ref,
         r_o, lw_o, k_o, v_o, kk_o, a_o, g_o) = refs
    t = pl.program_id(1)
    p = p_ref[0]
    prev = jnp.where(t == 0, 0.0, pp_ref[0, SUBLANES - 1:SUBLANES, :])
    row = lax.broadcasted_iota(jnp.int32, p.shape, 0)
    shifted = jnp.where(row == 0, prev, pltpu.roll(p, 1, 0))
    p = p + (shifted - p) * mu_ref[...]
    w = RWKV_WIDTH
    r, k, v = p[:, :w], p[:, w:2 * w], p[:, 2 * w:3 * w]
    o = 3 * w
    wd = p[:, o:o + DECAY_RANK]
    ad = p[:, o + DECAY_RANK:o + DECAY_RANK + ICLR_RANK]
    gd = p[:, o + DECAY_RANK + ICLR_RANK:]
    w_raw = -jax.nn.softplus(-(w0_ref[...] + _mm(jnp.tanh(wd), wup_ref[...]))) - 0.5
    a = jax.nn.sigmoid(a0_ref[...] + _mm(ad, aup_ref[...]))
    if has_vres:
        mix = jax.nn.sigmoid(vb_ref[...] + _mm(_mm(v, vd_ref[...]), vu_ref[...]))
        v = v + (vf_ref[0] - v) * mix
    r_o[0] = r
    lw_o[0] = -jnp.exp(w_raw)
    k_o[0] = k * (1.0 + (a - 1.0) * ka_ref[...])
    v_o[0] = v
    kk_o[0] = k * kk_ref[...]
    a_o[0] = a
    g_o[0] = _mm(jax.nn.sigmoid(gd), gup_ref[...])


def _rwkv_prep(p_r, mu, w_up, w0, a_up, a0, g_up, k_k, k_a, vres):
    b, t, _ = p_r.shape
    tt = min(ROW_TILE, t)
    has_vres = vres is not None
    seq = lambda n: pl.BlockSpec((1, tt, n), lambda i, j: (i, j, 0))
    prev = pl.BlockSpec((1, SUBLANES, RWKV_COLS),
                        lambda i, j: (i, jnp.maximum(j * (tt // SUBLANES) - 1, 0), 0))
    ins = [p_r, p_r, mu, w_up, w0, a_up, a0, g_up, k_k, k_a]
    specs = [seq(RWKV_COLS), prev] + [_const_spec(x.shape) for x in ins[2:]]
    if has_vres:
        v_first, v_down, v_upw, v_bias = vres
        ins += [v_first, v_down, v_upw, v_bias]
        specs += [seq(RWKV_WIDTH)] + [_const_spec(x.shape) for x in (v_down, v_upw, v_bias)]
    out = jax.ShapeDtypeStruct((b, t, RWKV_WIDTH), F32)
    return pl.pallas_call(
        functools.partial(_rwkv_prep_kernel, has_vres),
        grid=(b, t // tt),
        in_specs=specs,
        out_specs=[seq(RWKV_WIDTH)] * 7,
        out_shape=[out] * 7,
        compiler_params=pltpu.CompilerParams(
            dimension_semantics=("parallel", "parallel"), vmem_limit_bytes=VMEM_LIMIT_BYTES),
        name="rwkv_prep",
    )(*ins)


def _stack_heads(x3):
    lo = lax.broadcasted_iota(jnp.int32, x3.shape, 2) < RWKV_HEAD
    zero = jnp.zeros_like(x3)
    return jnp.concatenate([jnp.where(lo, x3, zero), jnp.where(lo, zero, x3)], axis=1)


def _head_sum(x):
    lo = lax.broadcasted_iota(jnp.int32, x.shape, 1) < RWKV_HEAD
    s0 = jnp.sum(jnp.where(lo, x, 0.0), axis=-1, keepdims=True)
    s1 = jnp.sum(jnp.where(lo, 0.0, x), axis=-1, keepdims=True)
    return jnp.where(lo, s0, s1)


def _wkv_kernel(r_ref, lw_ref, k_ref, v_ref, kk_ref, a_ref, gate_ref, rk_ref, lng_ref, lnb_ref,
                o_ref, h_ref):
    @pl.when(pl.program_id(2) == 0)
    def _():
        h_ref[...] = jnp.zeros_like(h_ref)

    nb, tt, _ = r_ref.shape
    flat = lambda ref: ref[...].reshape(nb * tt, LANES)
    r, lw, k, v, kkraw, a = (flat(x) for x in (r_ref, lw_ref, k_ref, v_ref, kk_ref, a_ref))
    ncs = tt // CHUNK
    nc = nb * ncs
    pp = 2 * CHUNK
    kk = kkraw * lax.rsqrt(_head_sum(kkraw * kkraw) + L2_EPS)
    c3 = lambda x: x.reshape(nc, CHUNK, LANES)
    r3, lw3, k3, v3, kk3, b3 = (c3(x) for x in (r, lw, k, v, kk, kk * a))

    row_c, col_c = _tri_masks(CHUNK)
    tril = jnp.broadcast_to((row_c >= col_c).astype(BF16), (nc, CHUNK, CHUNK))
    g = _mm_exact_lhs(tril, lw3, BNN)
    g_last = g[:, CHUNK - 1:CHUNK, :]
    e_ng = jnp.exp(-g)
    e_tail = jnp.exp(g_last - g)
    rt = r3 * jnp.exp(g)
    kt = (kk3 * jnp.exp(g - lw3)).astype(BF16)
    stack16 = lambda x: _stack_heads(x.astype(BF16))
    kh_s = stack16(k3 * e_ng)
    bh_s = stack16(b3 * e_ng)
    kbar_s = stack16(k3 * e_tail)
    bbar_s = stack16(b3 * e_tail)
    v_s = stack16(v3)

    row = lax.broadcasted_iota(jnp.int32, (CHUNK, LANES), 0)
    tcol = lax.broadcasted_iota(jnp.int32, (CHUNK, LANES), 1) & (CHUNK - 1)
    strict = row > tcol
    incl = row >= tcol
    sc = _mm(jnp.concatenate([kt, rt.astype(BF16)], axis=1),
             jnp.concatenate([kh_s, bh_s], axis=1), BNT)
    a_kk = jnp.where(strict, sc[:, :CHUNK, :LANES], 0.0)
    a_kb = jnp.where(strict, sc[:, :CHUNK, LANES:], 0.0)
    a_rk = jnp.where(incl, sc[:, CHUNK:, :LANES], 0.0)
    a_rb = jnp.where(incl, sc[:, CHUNK:, LANES:], 0.0)

    same = lambda s: (row ^ tcol) < s
    n = -jnp.where(same(2), a_kb, 0.0)
    s = 2
    while s < CHUNK:
        e = jnp.where(same(2 * s) & jnp.logical_not(same(s)), a_kb, 0.0)
        x = _mm(n, stack16(e), BNN)
        n = n - e - x - _mm(e + x, stack16(n), BNN)
        s *= 2

    av = _mm(jnp.concatenate([a_kk, a_rk], axis=1), v_s, BNN)
    akv = av[:, :CHUNK]
    rhs_s = jnp.concatenate([_stack_heads(kt), stack16(akv)], axis=2)
    w_d = kt.astype(F32) + _mm(n, rhs_s[:, :, :LANES], BNN)
    uv_d = akv + _mm(n, rhs_s[:, :, LANES:], BNN)
    wu_s = jnp.concatenate([stack16(w_d), stack16(uv_d)], axis=2)
    corr = _mm(a_rb, wu_s, BNN)
    qp = rt - corr[:, :, :LANES]
    yi = av[:, CHUNK:] - corr[:, :, LANES:]
    bw = _mm(bbar_s, wu_s, BTN)
    mc = bw[:, :, :LANES]
    dmat = _mm(kbar_s, v_s, BTN) - bw[:, :, LANES:]
    row_p, col_p = _tri_masks(pp)
    gcol = jnp.sum(jnp.where(row_p == col_p, jnp.exp(g_last), 0.0), axis=-1, keepdims=True)

    hs = [h_ref[b] for b in range(nb)]
    ys = [None] * nc
    for c in range(ncs):
        for b in range(nb):
            i = b * ncs + c
            z = _mm(jnp.concatenate([qp[i], mc[i]], axis=0), hs[b])
            ys[i] = z[:CHUNK] + yi[i]
            hs[b] = gcol[i] * hs[b] - z[CHUNK:] + dmat[i]
    for b in range(nb):
        h_ref[b] = hs[b]
    y = jnp.concatenate(ys, axis=0)

    inv_n = 1.0 / RWKV_HEAD
    d = y - _head_sum(y) * inv_n
    y = d * lax.rsqrt(_head_sum(d * d) * inv_n + GN_EPS) * lng_ref[...] + lnb_ref[...]
    bonus = _head_sum(r * k * rk_ref[...]) * v
    o_ref[...] = ((y + bonus) * flat(gate_ref)).reshape(nb, tt, LANES)


def _wkv(r, lw, k, v, kk, a, gate, r_k, ln_g, ln_b):
    b, t, w = r.shape
    tt = min(SEQ_TILE, t)
    nb = math.gcd(SEQ_BATCH, b)
    seq = pl.BlockSpec((nb, tt, LANES), lambda i, p, j: (i, j, p))
    par = pl.BlockSpec((1, LANES), lambda i, p, j: (0, p))
    return pl.pallas_call(
        _wkv_kernel,
        grid=(b // nb, w // LANES, t // tt),
        in_specs=[seq] * 7 + [par] * 3,
        out_specs=seq,
        out_shape=jax.ShapeDtypeStruct((b, t, w), F32),
        scratch_shapes=[pltpu.VMEM((nb, LANES, LANES), F32)],
        compiler_params=pltpu.CompilerParams(
            dimension_semantics=("parallel", "parallel", "arbitrary"),
            vmem_limit_bytes=VMEM_LIMIT_BYTES),
        name="wkv7",
    )(r, lw, k, v, kk, a, gate, r_k, ln_g, ln_b)


def _conv_silu(x_ref, prev_ref, w_ref, buf_ref, first):
    nb, tt, _ = x_ref.shape
    x = x_ref[...]
    buf_ref[:, :SUBLANES, :] = jnp.where(first, 0.0, prev_ref[...])
    buf_ref[:, SUBLANES:, :] = x
    acc = x * w_ref[CONV_WIDTH - 1:CONV_WIDTH, :]
    for d in range(1, CONV_WIDTH):
        tap = buf_ref[:, pl.ds(SUBLANES - d, tt), :]
        acc = acc + tap * w_ref[CONV_WIDTH - 1 - d:CONV_WIDTH - d, :]
    return jax.nn.silu(acc).reshape(nb * tt, LANES)


def _col_from_row(rowvec, eye):
    return jnp.sum(jnp.where(eye, rowvec, 0.0), axis=-1, keepdims=True)


def _pick_lane(x, idx):
    lane = lax.broadcasted_iota(jnp.int32, x.shape, x.ndim - 1)
    return jnp.sum(jnp.where(lane == idx, x, 0.0), axis=-1, keepdims=True)


def _gdn_kernel(q_ref, k_ref, v_ref, z_ref, qp_ref, kp_ref, vp_ref, cq_ref, ck_ref, cv_ref,
                abt_ref, alog_ref, dtb_ref, ng_ref, o_ref, s_ref, buf_ref):
    hd = pl.program_id(1)
    first = pl.program_id(2) == 0

    @pl.when(first)
    def _():
        s_ref[...] = jnp.zeros_like(s_ref)

    nb, tt, dk = q_ref.shape
    q = _conv_silu(q_ref, qp_ref, cq_ref, buf_ref.at[0], first)
    k = _conv_silu(k_ref, kp_ref, ck_ref, buf_ref.at[1], first)
    v = _conv_silu(v_ref, vp_ref, cv_ref, buf_ref.at[2], first)
    l2 = lambda x: x * lax.rsqrt(jnp.sum(x * x, axis=-1, keepdims=True) + L2_EPS)
    q = l2(q) * (GDN_HEAD ** -0.5)
    k = l2(k)

    pp = 2 * CHUNK
    nvs = tt // pp
    nv = nb * nvs
    row, col = _tri_masks(pp)
    same = (row ^ col) < CHUNK
    strict = (row > col) & same
    incl = (row >= col) & same
    eye = row == col

    a_log = _pick_lane(alog_ref[...], hd)
    dt_bias = _pick_lane(dtb_ref[...], hd)
    rows_of = lambda i: jnp.concatenate(
        [abt_ref[b, pl.ds(i, 1), :][:, j * pp:(j + 1) * pp] for b in range(nb) for j in range(nvs)],
        axis=0)
    alpha = rows_of(hd)
    beta_r = jax.nn.sigmoid(rows_of(hd + GDN_HEADS))
    g_r = -jnp.exp(a_log) * jax.nn.softplus(alpha + dt_bias)
    upper = ((row <= col) & same).astype(BF16)
    gc_r = _mm_exact_rhs(g_r, upper)
    col3 = lambda x: jnp.stack([_col_from_row(x[j:j + 1, :], eye) for j in range(nv)], axis=0)
    gc = col3(gc_r)
    beta3 = col3(beta_r)
    top = lax.broadcasted_iota(jnp.int32, (pp, 1), 0) < CHUNK
    g_last = jnp.where(top, gc[:, CHUNK - 1:CHUNK, :], gc[:, pp - 1:pp, :])
    decay = jnp.where(incl, jnp.exp(jnp.minimum(gc - gc_r[:, None, :], 0.0)), 0.0)

    c3 = lambda x: x.reshape(nv, pp, x.shape[-1])
    q3, k3, v3 = (c3(x) for x in (q, k, v))
    kb = k3 * beta3
    sc = _mm(jnp.concatenate([kb, q3], axis=1), k3, BNT)
    low = jnp.where(strict, sc[:, :pp] * decay, 0.0)
    a_qk = sc[:, pp:] * decay
    n = _unit_lower_inverse_minus_eye(low, row, col, CHUNK)
    e_gc = jnp.exp(gc)
    rhs = jnp.concatenate([kb * e_gc, v3 * beta3], axis=2)
    wu = rhs + _mm(n, rhs, BNN)
    corr = _mm(a_qk, wu, BNN)
    qp = q3 * e_gc - corr[:, :, :dk]
    yi = corr[:, :, dk:]
    kd = k3 * jnp.exp(g_last - gc)
    e_last = jnp.exp(g_last)

    per_chunk = lambda x: x.reshape(2 * nv, CHUNK, x.shape[-1])
    md = _mm(per_chunk(kd), per_chunk(wu), BTN)
    lhs = jnp.concatenate([per_chunk(qp), md[:, :, :dk]], axis=1).astype(BF16)
    yi2 = per_chunk(yi)
    e_last2 = per_chunk(e_last)[:, 0:1, :]

    ss = [s_ref[b] for b in range(nb)]
    outs = [None] * (2 * nv)
    for c in range(2 * nvs):
        for b in range(nb):
            i = b * 2 * nvs + c
            z = _mm(lhs[i], ss[b])
            outs[i] = z[:CHUNK] + yi2[i]
            ss[b] = e_last2[i] * ss[b] - z[CHUNK:] + md[i, :, dk:]
    for b in range(nb):
        s_ref[b] = ss[b]
    o = jnp.concatenate(outs, axis=0)
    o = o * lax.rsqrt(jnp.mean(o * o, axis=-1, keepdims=True) + GATED_NORM_EPS)
    gate = jax.nn.silu(z_ref[...].reshape(nb * tt, dk))
    o_ref[...] = (o * ng_ref[...] * gate).reshape(nb, tt, dk)


def _gdn(p_g, p_abt, conv_w, a_log8, dt_bias8, norm_g):
    b, t, _ = p_g.shape
    tt = min(SEQ_TILE, t)
    nb = math.gcd(SEQ_BATCH, b)
    nh = GDN_HEADS
    seq = lambda off: pl.BlockSpec((nb, tt, LANES), lambda i, h, j: (i, j, off + h))
    prev = lambda off: pl.BlockSpec(
        (nb, SUBLANES, LANES),
        lambda i, h, j: (i, jnp.maximum(j * (tt // SUBLANES) - 1, 0), off + h))
    cw = lambda off: pl.BlockSpec((CONV_WIDTH, LANES), lambda i, h, j: (0, off + h))
    small = lambda n: pl.BlockSpec((1, n), lambda i, h, j: (0, 0))
    return pl.pallas_call(
        _gdn_kernel,
        grid=(b // nb, nh, t // tt),
        in_specs=[seq(0), seq(nh), seq(2 * nh), seq(3 * nh), prev(0), prev(nh), prev(2 * nh),
                  cw(0), cw(nh), cw(2 * nh),
                  pl.BlockSpec((nb, AB_COLS, tt), lambda i, h, j: (i, 0, j)),
                  small(AB_COLS), small(AB_COLS), small(LANES)],
        out_specs=pl.BlockSpec((nb, tt, LANES), lambda i, h, j: (i, j, h)),
        out_shape=jax.ShapeDtypeStruct((b, t, GDN_WIDTH), F32),
        scratch_shapes=[pltpu.VMEM((nb, GDN_HEAD, GDN_HEAD), F32),
                        pltpu.VMEM((3, nb, SUBLANES + tt, LANES), F32)],
        compiler_params=pltpu.CompilerParams(
            dimension_semantics=("parallel", "parallel", "arbitrary"),
            vmem_limit_bytes=VMEM_LIMIT_BYTES),
        name="gdn",
    )(p_g, p_g, p_g, p_g, p_g, p_g, p_g, conv_w, conv_w, conv_w, p_abt, a_log8, dt_bias8, norm_g)


def _out_mlp_kernel(final, ya_ref, yb_ref, h_ref, wo_ref, g2_ref, up_ref, dn_ref, gf_ref, o_ref):
    w = RWKV_WIDTH
    h = h_ref[...]
    h = h + jnp.dot(ya_ref[...].astype(BF16), wo_ref[:w, :], preferred_element_type=F32)
    h = h + jnp.dot(yb_ref[...].astype(BF16), wo_ref[w:, :], preferred_element_type=F32)
    xb = _rmsnorm(h, g2_ref[...], NORM_EPS).astype(BF16)
    hid = jnp.dot(xb, up_ref[...], preferred_element_type=F32)
    hid = jnp.square(jnp.maximum(hid, 0.0)).astype(BF16)
    acc = h + jnp.dot(hid, dn_ref[...], preferred_element_type=F32)
    if final:
        acc = _rmsnorm(acc, gf_ref[...], NORM_EPS)
    o_ref[...] = acc


def _out_mlp(ya, yb, h2d, w_out, g2, up, dn, gf, final):
    m = h2d.shape[0]
    tm = min(ROW_TILE, m)
    row = lambda n: pl.BlockSpec((tm, n), lambda i: (i, 0))
    return pl.pallas_call(
        functools.partial(_out_mlp_kernel, final),
        grid=(m // tm,),
        in_specs=[row(RWKV_WIDTH), row(GDN_WIDTH), row(D_MODEL), _const_spec(w_out.shape),
                  _const_spec(g2.shape), _const_spec(up.shape), _const_spec(dn.shape),
                  _const_spec(gf.shape)],
        out_specs=row(D_MODEL),
        out_shape=jax.ShapeDtypeStruct((m, D_MODEL), F32),
        compiler_params=pltpu.CompilerParams(
            dimension_semantics=("parallel",), vmem_limit_bytes=VMEM_LIMIT_BYTES),
        name="out_mlp",
    )(ya, yb, h2d, w_out, g2, up, dn, gf)


def kernel(x, norm1_g, w_in, shift_mu, rw_w_up, rw_w0, rw_a_up, rw_a0, rw_g_up, rw_k_k, rw_k_a,
           rw_r_k, rw_ln_g, rw_ln_b, rw_vres_down, rw_vres_up, rw_vres_b, gdn_conv, gdn_A_log,
           gdn_dt_bias, gdn_norm_g, w_out, norm2_g, mlp_up, mlp_down, final_g):
    b, t, d = x.shape
    depth = w_in.shape[0]
    m = b * t
    row = lambda v: v.reshape(1, -1)
    pad8 = lambda v: jnp.pad(v, (0, AB_COLS - v.shape[0])).reshape(1, AB_COLS)

    h = x.reshape(m, d)
    v_first = None
    for l in range(depth):
        wl = w_in[l].astype(BF16)
        wr = wl[:, :RWKV_COLS]
        wg = wl[:, RWKV_COLS:RWKV_COLS + GDN_MAIN_COLS]
        wab = jnp.pad(wl[:, RWKV_COLS + GDN_MAIN_COLS:], ((0, 0), (0, LANES - AB_COLS)))
        p_r, p_g, p_ab = _inproj(h, row(norm1_g[l]), wr, wg, wab)
        p_r = p_r.reshape(b, t, RWKV_COLS)
        p_g = p_g.reshape(b, t, GDN_MAIN_COLS)
        p_abt = jnp.swapaxes(p_ab.reshape(b, t, AB_COLS), 1, 2)

        vres = None
        if l > 0:
            vres = (v_first, rw_vres_down[l - 1].astype(BF16), rw_vres_up[l - 1].astype(BF16),
                    row(rw_vres_b[l - 1]))
        r, lw, k, v, kk, a, gate = _rwkv_prep(
            p_r, row(shift_mu[l]), rw_w_up[l].astype(BF16), row(rw_w0[l]),
            rw_a_up[l].astype(BF16), row(rw_a0[l]), rw_g_up[l].astype(BF16),
            row(rw_k_k[l]), row(rw_k_a[l]), vres)
        if l == 0:
            v_first = v
        y_a = _wkv(r, lw, k, v, kk, a, gate, row(rw_r_k[l]), row(rw_ln_g[l]), row(rw_ln_b[l]))
        y_b = _gdn(p_g, p_abt, gdn_conv[l], pad8(gdn_A_log[l]), pad8(gdn_dt_bias[l]),
                   row(gdn_norm_g[l]))
        h = _out_mlp(y_a.reshape(m, RWKV_WIDTH), y_b.reshape(m, GDN_WIDTH), h,
                     w_out[l].astype(BF16), row(norm2_g[l]), mlp_up[l].astype(BF16),
                     mlp_down[l].astype(BF16), row(final_g), l == depth - 1)
    return h.reshape(b, t, d)
```

```python
import functools
import math

import jax
import jax.numpy as jnp
from jax import lax
from jax.experimental import pallas as pl
from jax.experimental.pallas import tpu as pltpu

F32 = jnp.float32
BF16 = jnp.bfloat16

D_MODEL = 1024
RWKV_WIDTH = 512
RWKV_HEAD = 64
DECAY_RANK = 64
ICLR_RANK = 64
GATE_RANK = 128
GN_EPS = 64e-5
GDN_WIDTH = 512
GDN_HEAD = 128
GDN_HEADS = 4
CONV_WIDTH = 4
CHUNK = 64
D_FF = 4 * D_MODEL
NORM_EPS = 1e-5
GATED_NORM_EPS = 1e-6
L2_EPS = 1e-6
RWKV_COLS = 3 * RWKV_WIDTH + DECAY_RANK + ICLR_RANK + GATE_RANK
GDN_MAIN_COLS = 4 * GDN_WIDTH
AB_COLS = 2 * GDN_HEADS

LANES = 128
SUBLANES = 8
VMEM_LIMIT_BYTES = 56 * 1024 * 1024

ROW_TILE = 512
SEQ_TILE = 256
SEQ_BATCH = 8

NN = (((1,), (0,)), ((), ()))
NT = (((1,), (1,)), ((), ()))
BNN = (((2,), (1,)), ((0,), (0,)))
BNT = (((2,), (2,)), ((0,), (0,)))
BTN = (((1,), (1,)), ((0,), (0,)))
TN = (((0,), (0,)), ((), ()))


def _mm(a, b, dims=NN):
    return lax.dot_general(a.astype(BF16), b.astype(BF16), dims, preferred_element_type=F32)


def _mm_exact_rhs(a, b_bf16, dims=NN):
    a1 = a.astype(BF16)
    r1 = a - a1.astype(F32)
    a2 = r1.astype(BF16)
    a3 = (r1 - a2.astype(F32)).astype(BF16)
    d = lambda y: lax.dot_general(y, b_bf16, dims, preferred_element_type=F32)
    return d(a1) + (d(a2) + d(a3))


def _rmsnorm(x, g, eps):
    return x * lax.rsqrt(jnp.mean(x * x, axis=-1, keepdims=True) + eps) * g


def _tri_masks(c):
    row = lax.broadcasted_iota(jnp.int32, (c, c), 0)
    col = lax.broadcasted_iota(jnp.int32, (c, c), 1)
    return row, col


def _unit_lower_inverse_minus_eye(low, row, col, block):
    same = lambda s: (row ^ col) < s
    n = -jnp.where(same(2), low, 0.0)
    s = 2
    while s < block:
        e = jnp.where(same(2 * s) & jnp.logical_not(same(s)), low, 0.0)
        x = _mm(n, e, BNN)
        n = n - e - x - _mm(e + x, n, BNN)
        s *= 2
    return n


def _const_spec(shape):
    nd = len(shape)
    return pl.BlockSpec(shape, lambda *_: (0,) * nd, pipeline_mode=pl.Buffered(1))


def _inproj_kernel(has_vres, *refs):
    if has_vres:
        (x_ref, g_ref, wr_ref, wg_ref, wab_ref,
         mu_ref, wup_ref, w0_ref, aup_ref, a0_ref, gup_ref, kk_ref, ka_ref,
         vf_ref, vd_ref, vu_ref, vb_ref,
         r_o, lw_o, k_o, v_o, kk_o, a_o, g_o, pg_ref, pab_ref, p_ref) = refs
    else:
        (x_ref, g_ref, wr_ref, wg_ref, wab_ref,
         mu_ref, wup_ref, w0_ref, aup_ref, a0_ref, gup_ref, kk_ref, ka_ref,
         r_o, lw_o, k_o, v_o, kk_o, a_o, g_o, pg_ref, pab_ref, p_ref) = refs
    tm = x_ref.shape[1]
    xb = _rmsnorm(x_ref[0], g_ref[...], NORM_EPS).astype(BF16)
    pg_ref[0] = jnp.dot(xb, wg_ref[...], preferred_element_type=F32)
    pab_ref[0] = jnp.dot(xb, wab_ref[...], preferred_element_type=F32)[:, :AB_COLS]

    @pl.when(pl.program_id(1) == 0)
    def _():
        p_ref[:SUBLANES, :] = jnp.zeros((SUBLANES, RWKV_COLS), F32)

    @pl.when(pl.program_id(1) != 0)
    def _():
        p_ref[:SUBLANES, :] = p_ref[tm:, :]

    p = jnp.dot(xb, wr_ref[...], preferred_element_type=F32)
    p_ref[SUBLANES:, :] = p
    p = p + (p_ref[pl.ds(SUBLANES - 1, tm), :] - p) * mu_ref[...]
    w = RWKV_WIDTH
    r, k, v = p[:, :w], p[:, w:2 * w], p[:, 2 * w:3 * w]
    o = 3 * w
    wd = p[:, o:o + DECAY_RANK]
    ad = p[:, o + DECAY_RANK:o + DECAY_RANK + ICLR_RANK]
    gd = p[:, o + DECAY_RANK + ICLR_RANK:]
    w_raw = -jax.nn.softplus(-(w0_ref[...] + _mm(jnp.tanh(wd), wup_ref[...]))) - 0.5
    a = jax.nn.sigmoid(a0_ref[...] + _mm(ad, aup_ref[...]))
    if has_vres:
        mix = jax.nn.sigmoid(vb_ref[...] + _mm(_mm(v, vd_ref[...]), vu_ref[...]))
        v = v + (vf_ref[0] - v) * mix
    r_o[0] = r
    lw_o[0] = -jnp.exp(w_raw)
    k_o[0] = k * (1.0 + (a - 1.0) * ka_ref[...])
    v_o[0] = v
    kk_o[0] = k * kk_ref[...]
    a_o[0] = a
    g_o[0] = _mm(jax.nn.sigmoid(gd), gup_ref[...])


def _inproj(h, g, wr, wg, wab, mu, w_up, w0, a_up, a0, g_up, k_k, k_a, vres):
    b, t, _ = h.shape
    tm = min(ROW_TILE, t)
    has_vres = vres is not None
    seq = lambda n: pl.BlockSpec((1, tm, n), lambda i, j: (i, j, 0))
    ins = [h, g, wr, wg, wab, mu, w_up, w0, a_up, a0, g_up, k_k, k_a]
    specs = [seq(D_MODEL)] + [_const_spec(x.shape) for x in ins[1:]]
    if has_vres:
        v_first, v_down, v_upw, v_bias = vres
        ins += [v_first, v_down, v_upw, v_bias]
        specs += [seq(RWKV_WIDTH)] + [_const_spec(x.shape) for x in (v_down, v_upw, v_bias)]
    out = lambda n: jax.ShapeDtypeStruct((b, t, n), F32)
    return pl.pallas_call(
        functools.partial(_inproj_kernel, has_vres),
        grid=(b, t // tm),
        in_specs=specs,
        out_specs=[seq(RWKV_WIDTH)] * 7 + [seq(GDN_MAIN_COLS), seq(AB_COLS)],
        out_shape=[out(RWKV_WIDTH)] * 7 + [out(GDN_MAIN_COLS), out(AB_COLS)],
        scratch_shapes=[pltpu.VMEM((SUBLANES + tm, RWKV_COLS), F32)],
        compiler_params=pltpu.CompilerParams(
            dimension_semantics=("parallel", "arbitrary"), vmem_limit_bytes=VMEM_LIMIT_BYTES),
        name="inproj",
    )(*ins)


def _stack_heads(x3):
    lo = lax.broadcasted_iota(jnp.int32, x3.shape, 2) < RWKV_HEAD
    zero = jnp.zeros_like(x3)
    return jnp.concatenate([jnp.where(lo, x3, zero), jnp.where(lo, zero, x3)], axis=1)


def _head_sum(x):
    lo = lax.broadcasted_iota(jnp.int32, x.shape, 1) < RWKV_HEAD
    s0 = jnp.sum(jnp.where(lo, x, 0.0), axis=-1, keepdims=True)
    s1 = jnp.sum(jnp.where(lo, 0.0, x), axis=-1, keepdims=True)
    return jnp.where(lo, s0, s1)


def _wkv_kernel(r_ref, lw_ref, k_ref, v_ref, kk_ref, a_ref, gate_ref, rk_ref, lng_ref, lnb_ref,
                o_ref, h_ref):
    @pl.when(pl.program_id(2) == 0)
    def _():
        h_ref[...] = jnp.zeros_like(h_ref)

    nb, tt, _ = r_ref.shape
    flat = lambda ref: ref[...].reshape(nb * tt, LANES)
    r, lw, k, v, kkraw, a = (flat(x) for x in (r_ref, lw_ref, k_ref, v_ref, kk_ref, a_ref))
    ncs = tt // CHUNK
    nc = nb * ncs
    pp = 2 * CHUNK
    kk = kkraw * lax.rsqrt(_head_sum(kkraw * kkraw) + L2_EPS)
    c3 = lambda x: x.reshape(nc, CHUNK, LANES)
    in_chunk = lax.broadcasted_iota(jnp.int32, lw.shape, 0) & (CHUNK - 1)
    gsum = lw
    step = 1
    while step < CHUNK:
        gsum = gsum + jnp.where(in_chunk >= step, pltpu.roll(gsum, step, 0), 0.0)
        step *= 2
    r3, lw3, k3, v3, kk3, b3, g = (c3(x) for x in (r, lw, k, v, kk, kk * a, gsum))
    g_last = g[:, CHUNK - 1:CHUNK, :]
    e_ng = jnp.exp(-g)
    e_tail = jnp.exp(g_last - g)
    rt = r3 * jnp.exp(g)
    kt = (kk3 * jnp.exp(g - lw3)).astype(BF16)
    stack16 = lambda x: _stack_heads(x.astype(BF16))
    kh_s = stack16(k3 * e_ng)
    bh_s = stack16(b3 * e_ng)
    kbar_s = stack16(k3 * e_tail)
    bbar_s = stack16(b3 * e_tail)
    v_s = stack16(v3)

    row = lax.broadcasted_iota(jnp.int32, (CHUNK, LANES), 0)
    tcol = lax.broadcasted_iota(jnp.int32, (CHUNK, LANES), 1) & (CHUNK - 1)
    strict = row > tcol
    incl = row >= tcol
    sc = _mm(jnp.concatenate([kt, rt.astype(BF16)], axis=1),
             jnp.concatenate([kh_s, bh_s], axis=1), BNT)
    a_kk = jnp.where(strict, sc[:, :CHUNK, :LANES], 0.0)
    a_kb = jnp.where(strict, sc[:, :CHUNK, LANES:], 0.0)
    a_rk = jnp.where(incl, sc[:, CHUNK:, :LANES], 0.0)
    a_rb = jnp.where(incl, sc[:, CHUNK:, LANES:], 0.0)

    same = lambda s: (row ^ tcol) < s
    n = -jnp.where(same(2), a_kb, 0.0)
    s = 2
    while s < CHUNK:
        e = jnp.where(same(2 * s) & jnp.logical_not(same(s)), a_kb, 0.0)
        x = _mm(n, stack16(e), BNN)
        n = n - e - x - _mm(e + x, stack16(n), BNN)
        s *= 2

    av = _mm(jnp.concatenate([a_kk, a_rk], axis=1), v_s, BNN)
    akv = av[:, :CHUNK]
    rhs_s = jnp.concatenate([_stack_heads(kt), stack16(akv)], axis=2)
    w_d = kt.astype(F32) + _mm(n, rhs_s[:, :, :LANES], BNN)
    uv_d = akv + _mm(n, rhs_s[:, :, LANES:], BNN)
    wu_s = jnp.concatenate([stack16(w_d), stack16(uv_d)], axis=2)
    corr = _mm(a_rb, wu_s, BNN)
    qp = rt - corr[:, :, :LANES]
    yi = av[:, CHUNK:] - corr[:, :, LANES:]
    bw = _mm(bbar_s, wu_s, BTN)
    mc = bw[:, :, :LANES]
    dmat = _mm(kbar_s, v_s, BTN) - bw[:, :, LANES:]
    row_p, col_p = _tri_masks(pp)
    gcol = jnp.sum(jnp.where(row_p == col_p, jnp.exp(g_last), 0.0), axis=-1, keepdims=True)

    hs = [h_ref[b] for b in range(nb)]
    ys = [None] * nc
    for c in range(ncs):
        for b in range(nb):
            i = b * ncs + c
            z = _mm(jnp.concatenate([qp[i], mc[i]], axis=0), hs[b])
            ys[i] = z[:CHUNK] + yi[i]
            hs[b] = gcol[i] * hs[b] - z[CHUNK:] + dmat[i]
    for b in range(nb):
        h_ref[b] = hs[b]
    y = jnp.concatenate(ys, axis=0)

    inv_n = 1.0 / RWKV_HEAD
    d = y - _head_sum(y) * inv_n
    y = d * lax.rsqrt(_head_sum(d * d) * inv_n + GN_EPS) * lng_ref[...] + lnb_ref[...]
    bonus = _head_sum(r * k * rk_ref[...]) * v
    o_ref[...] = ((y + bonus) * flat(gate_ref)).reshape(nb, tt, LANES)


def _wkv(r, lw, k, v, kk, a, gate, r_k, ln_g, ln_b):
    b, t, w = r.shape
    tt = min(SEQ_TILE, t)
    nb = math.gcd(SEQ_BATCH, b)
    seq = pl.BlockSpec((nb, tt, LANES), lambda i, p, j: (i, j, p))
    par = pl.BlockSpec((1, LANES), lambda i, p, j: (0, p))
    return pl.pallas_call(
        _wkv_kernel,
        grid=(b // nb, w // LANES, t // tt),
        in_specs=[seq] * 7 + [par] * 3,
        out_specs=seq,
        out_shape=jax.ShapeDtypeStruct((b, t, w), F32),
        scratch_shapes=[pltpu.VMEM((nb, LANES, LANES), F32)],
        compiler_params=pltpu.CompilerParams(
            dimension_semantics=("parallel", "parallel", "arbitrary"),
            vmem_limit_bytes=VMEM_LIMIT_BYTES),
        name="wkv7",
    )(r, lw, k, v, kk, a, gate, r_k, ln_g, ln_b)


def _conv_silu(x_ref, prev_ref, w_ref, buf_ref, first):
    nb, tt, _ = x_ref.shape
    x = x_ref[...]
    buf_ref[:, :SUBLANES, :] = jnp.where(first, 0.0, prev_ref[...])
    buf_ref[:, SUBLANES:, :] = x
    acc = x * w_ref[CONV_WIDTH - 1:CONV_WIDTH, :]
    for d in range(1, CONV_WIDTH):
        tap = buf_ref[:, pl.ds(SUBLANES - d, tt), :]
        acc = acc + tap * w_ref[CONV_WIDTH - 1 - d:CONV_WIDTH - d, :]
    return jax.nn.silu(acc).reshape(nb * tt, LANES)


def _col_from_row(rowvec, eye):
    return jnp.sum(jnp.where(eye, rowvec, 0.0), axis=-1, keepdims=True)


def _pick_lane(x, idx):
    lane = lax.broadcasted_iota(jnp.int32, x.shape, x.ndim - 1)
    return jnp.sum(jnp.where(lane == idx, x, 0.0), axis=-1, keepdims=True)


def _gdn_kernel(q_ref, k_ref, v_ref, z_ref, qp_ref, kp_ref, vp_ref, cq_ref, ck_ref, cv_ref,
                abt_ref, alog_ref, dtb_ref, ng_ref, o_ref, s_ref, buf_ref):
    hd = pl.program_id(1)
    first = pl.program_id(2) == 0

    @pl.when(first)
    def _():
        s_ref[...] = jnp.zeros_like(s_ref)

    nb, tt, dk = q_ref.shape
    q = _conv_silu(q_ref, qp_ref, cq_ref, buf_ref.at[0], first)
    k = _conv_silu(k_ref, kp_ref, ck_ref, buf_ref.at[1], first)
    v = _conv_silu(v_ref, vp_ref, cv_ref, buf_ref.at[2], first)
    l2 = lambda x: x * lax.rsqrt(jnp.sum(x * x, axis=-1, keepdims=True) + L2_EPS)
    q = l2(q) * (GDN_HEAD ** -0.5)
    k = l2(k)

    pp = 2 * CHUNK
    nvs = tt // pp
    nv = nb * nvs
    row, col = _tri_masks(pp)
    same = (row ^ col) < CHUNK
    strict = (row > col) & same
    incl = (row >= col) & same
    eye = row == col

    a_log = _pick_lane(alog_ref[...], hd)
    dt_bias = _pick_lane(dtb_ref[...], hd)
    rows_of = lambda i: jnp.concatenate(
        [abt_ref[b, pl.ds(i, 1), :][:, j * pp:(j + 1) * pp] for b in range(nb) for j in range(nvs)],
        axis=0)
    alpha = rows_of(hd)
    beta_r = jax.nn.sigmoid(rows_of(hd + GDN_HEADS))
    g_r = -jnp.exp(a_log) * jax.nn.softplus(alpha + dt_bias)
    upper = ((row <= col) & same).astype(BF16)
    gc_r = _mm_exact_rhs(g_r, upper)
    col3 = lambda x: jnp.stack([_col_from_row(x[j:j + 1, :], eye) for j in range(nv)], axis=0)
    gc = col3(gc_r)
    beta3 = col3(beta_r)
    top = lax.broadcasted_iota(jnp.int32, (pp, 1), 0) < CHUNK
    g_last = jnp.where(top, gc[:, CHUNK - 1:CHUNK, :], gc[:, pp - 1:pp, :])
    decay = jnp.where(incl, jnp.exp(jnp.minimum(gc - gc_r[:, None, :], 0.0)), 0.0)

    c3 = lambda x: x.reshape(nv, pp, x.shape[-1])
    q3, k3, v3 = (c3(x) for x in (q, k, v))
    kb = k3 * beta3
    sc = _mm(jnp.concatenate([kb, q3], axis=1), k3, BNT)
    low = jnp.where(strict, sc[:, :pp] * decay, 0.0)
    a_qk = sc[:, pp:] * decay
    n = _unit_lower_inverse_minus_eye(low, row, col, CHUNK)
    e_gc = jnp.exp(gc)
    rhs = jnp.concatenate([kb * e_gc, v3 * beta3], axis=2)
    wu = rhs + _mm(n, rhs, BNN)
    corr = _mm(a_qk, wu, BNN)
    qp = q3 * e_gc - corr[:, :, :dk]
    yi = corr[:, :, dk:]
    kd = k3 * jnp.exp(g_last - gc)
    e_last = jnp.exp(g_last)

    per_chunk = lambda x: x.reshape(2 * nv, CHUNK, x.shape[-1])
    md = _mm(per_chunk(kd), per_chunk(wu), BTN)
    lhs = jnp.concatenate([per_chunk(qp), md[:, :, :dk]], axis=1).astype(BF16)
    yi2 = per_chunk(yi)
    e_last2 = per_chunk(e_last)[:, 0:1, :]

    ss = [s_ref[b] for b in range(nb)]
    outs = [None] * (2 * nv)
    for c in range(2 * nvs):
        for b in range(nb):
            i = b * 2 * nvs + c
            z = _mm(lhs[i], ss[b])
            outs[i] = z[:CHUNK] + yi2[i]
            ss[b] = e_last2[i] * ss[b] - z[CHUNK:] + md[i, :, dk:]
    for b in range(nb):
        s_ref[b] = ss[b]
    o = jnp.concatenate(outs, axis=0)
    o = o * lax.rsqrt(jnp.mean(o * o, axis=-1, keepdims=True) + GATED_NORM_EPS)
    gate = jax.nn.silu(z_ref[...].reshape(nb * tt, dk))
    o_ref[...] = (o * ng_ref[...] * gate).reshape(nb, tt, dk)


def _gdn(p_g, p_abt, conv_w, a_log8, dt_bias8, norm_g):
    b, t, _ = p_g.shape
    tt = min(SEQ_TILE, t)
    nb = math.gcd(SEQ_BATCH, b)
    nh = GDN_HEADS
    seq = lambda off: pl.BlockSpec((nb, tt, LANES), lambda i, h, j: (i, j, off + h))
    prev = lambda off: pl.BlockSpec(
        (nb, SUBLANES, LANES),
        lambda i, h, j: (i, jnp.maximum(j * (tt // SUBLANES) - 1, 0), off + h))
    cw = lambda off: pl.BlockSpec((CONV_WIDTH, LANES), lambda i, h, j: (0, off + h))
    small = lambda n: pl.BlockSpec((1, n), lambda i, h, j: (0, 0))
    return pl.pallas_call(
        _gdn_kernel,
        grid=(b // nb, nh, t // tt),
        in_specs=[seq(0), seq(nh), seq(2 * nh), seq(3 * nh), prev(0), prev(nh), prev(2 * nh),
                  cw(0), cw(nh), cw(2 * nh),
                  pl.BlockSpec((nb, AB_COLS, tt), lambda i, h, j: (i, 0, j)),
                  small(AB_COLS), small(AB_COLS), small(LANES)],
        out_specs=pl.BlockSpec((nb, tt, LANES), lambda i, h, j: (i, j, h)),
        out_shape=jax.ShapeDtypeStruct((b, t, GDN_WIDTH), F32),
        scratch_shapes=[pltpu.VMEM((nb, GDN_HEAD, GDN_HEAD), F32),
                        pltpu.VMEM((3, nb, SUBLANES + tt, LANES), F32)],
        compiler_params=pltpu.CompilerParams(
            dimension_semantics=("parallel", "parallel", "arbitrary"),
            vmem_limit_bytes=VMEM_LIMIT_BYTES),
        name="gdn",
    )(p_g, p_g, p_g, p_g, p_g, p_g, p_g, conv_w, conv_w, conv_w, p_abt, a_log8, dt_bias8, norm_g)


def _out_mlp_kernel(final, ya_ref, yb_ref, h_ref, wo_ref, g2_ref, up_ref, dn_ref, gf_ref, o_ref):
    w = RWKV_WIDTH
    h = h_ref[...]
    h = h + jnp.dot(ya_ref[...].astype(BF16), wo_ref[:w, :], preferred_element_type=F32)
    h = h + jnp.dot(yb_ref[...].astype(BF16), wo_ref[w:, :], preferred_element_type=F32)
    xb = _rmsnorm(h, g2_ref[...], NORM_EPS).astype(BF16)
    hid = jnp.dot(xb, up_ref[...], preferred_element_type=F32)
    hid = jnp.square(jnp.maximum(hid, 0.0)).astype(BF16)
    acc = h + jnp.dot(hid, dn_ref[...], preferred_element_type=F32)
    if final:
        acc = _rmsnorm(acc, gf_ref[...], NORM_EPS)
    o_ref[...] = acc


def _out_mlp(ya, yb, h2d, w_out, g2, up, dn, gf, final):
    m = h2d.shape[0]
    tm = min(ROW_TILE, m)
    row = lambda n: pl.BlockSpec((tm, n), lambda i: (i, 0))
    return pl.pallas_call(
        functools.partial(_out_mlp_kernel, final),
        grid=(m // tm,),
        in_specs=[row(RWKV_WIDTH), row(GDN_WIDTH), row(D_MODEL), _const_spec(w_out.shape),
                  _const_spec(g2.shape), _const_spec(up.shape), _const_spec(dn.shape),
                  _const_spec(gf.shape)],
        out_specs=row(D_MODEL),
        out_shape=jax.ShapeDtypeStruct((m, D_MODEL), F32),
        compiler_params=pltpu.CompilerParams(
            dimension_semantics=("parallel",), vmem_limit_bytes=VMEM_LIMIT_BYTES),
        name="out_mlp",
    )(ya, yb, h2d, w_out, g2, up, dn, gf)


def kernel(x, norm1_g, w_in, shift_mu, rw_w_up, rw_w0, rw_a_up, rw_a0, rw_g_up, rw_k_k, rw_k_a,
           rw_r_k, rw_ln_g, rw_ln_b, rw_vres_down, rw_vres_up, rw_vres_b, gdn_conv, gdn_A_log,
           gdn_dt_bias, gdn_norm_g, w_out, norm2_g, mlp_up, mlp_down, final_g):
    b, t, d = x.shape
    depth = w_in.shape[0]
    m = b * t
    row = lambda v: v.reshape(1, -1)
    pad8 = lambda v: jnp.pad(v, (0, AB_COLS - v.shape[0])).reshape(1, AB_COLS)

    h = x.reshape(m, d)
    v_first = None
    for l in range(depth):
        wl = w_in[l].astype(BF16)
        wr = wl[:, :RWKV_COLS]
        wg = wl[:, RWKV_COLS:RWKV_COLS + GDN_MAIN_COLS]
        wab = jnp.pad(wl[:, RWKV_COLS + GDN_MAIN_COLS:], ((0, 0), (0, LANES - AB_COLS)))
        vres = None
        if l > 0:
            vres = (v_first, rw_vres_down[l - 1].astype(BF16), rw_vres_up[l - 1].astype(BF16),
                    row(rw_vres_b[l - 1]))
        r, lw, k, v, kk, a, gate, p_g, p_ab = _inproj(
            h.reshape(b, t, d), row(norm1_g[l]), wr, wg, wab,
            row(shift_mu[l]), rw_w_up[l].astype(BF16), row(rw_w0[l]),
            rw_a_up[l].astype(BF16), row(rw_a0[l]), rw_g_up[l].astype(BF16),
            row(rw_k_k[l]), row(rw_k_a[l]), vres)
        if l == 0:
            v_first = v
        p_abt = jnp.swapaxes(p_ab, 1, 2)
        y_a = _wkv(r, lw, k, v, kk, a, gate, row(rw_r_k[l]), row(rw_ln_g[l]), row(rw_ln_b[l]))
        y_b = _gdn(p_g, p_abt, gdn_conv[l], pad8(gdn_A_log[l]), pad8(gdn_dt_bias[l]),
                   row(gdn_norm_g[l]))
        h = _out_mlp(y_a.reshape(m, RWKV_WIDTH), y_b.reshape(m, GDN_WIDTH), h,
                     w_out[l].astype(BF16), row(norm2_g[l]), mlp_up[l].astype(BF16),
                     mlp_down[l].astype(BF16), row(final_g), l == depth - 1)
    return h.reshape(b, t, d)
```

```python
import functools
import math

import jax
import jax.numpy as jnp
from jax import lax
from jax.experimental import pallas as pl
from jax.experimental.pallas import tpu as pltpu

F32 = jnp.float32
BF16 = jnp.bfloat16

D_MODEL = 1024
RWKV_WIDTH = 512
RWKV_HEAD = 64
DECAY_RANK = 64
ICLR_RANK = 64
GATE_RANK = 128
GN_EPS = 64e-5
GDN_WIDTH = 512
GDN_HEAD = 128
GDN_HEADS = 4
CONV_WIDTH = 4
CHUNK = 64
D_FF = 4 * D_MODEL
NORM_EPS = 1e-5
GATED_NORM_EPS = 1e-6
L2_EPS = 1e-6
RWKV_COLS = 3 * RWKV_WIDTH + DECAY_RANK + ICLR_RANK + GATE_RANK
GDN_MAIN_COLS = 4 * GDN_WIDTH
AB_COLS = 2 * GDN_HEADS

LANES = 128
SUBLANES = 8
VMEM_LIMIT_BYTES = 56 * 1024 * 1024

ROW_TILE = 512
SEQ_TILE = 256
SEQ_BATCH = 8

NN = (((1,), (0,)), ((), ()))
NT = (((1,), (1,)), ((), ()))
BNN = (((2,), (1,)), ((0,), (0,)))
BNT = (((2,), (2,)), ((0,), (0,)))
BTN = (((1,), (1,)), ((0,), (0,)))
TN = (((0,), (0,)), ((), ()))


def _mm(a, b, dims=NN):
    return lax.dot_general(a.astype(BF16), b.astype(BF16), dims, preferred_element_type=F32)


def _mm_exact_rhs(a, b_bf16, dims=NN):
    a1 = a.astype(BF16)
    r1 = a - a1.astype(F32)
    a2 = r1.astype(BF16)
    a3 = (r1 - a2.astype(F32)).astype(BF16)
    d = lambda y: lax.dot_general(y, b_bf16, dims, preferred_element_type=F32)
    return d(a1) + (d(a2) + d(a3))


def _rmsnorm(x, g, eps):
    return x * lax.rsqrt(jnp.mean(x * x, axis=-1, keepdims=True) + eps) * g


def _tri_masks(c):
    row = lax.broadcasted_iota(jnp.int32, (c, c), 0)
    col = lax.broadcasted_iota(jnp.int32, (c, c), 1)
    return row, col


def _unit_lower_inverse_minus_eye(low, row, col, block):
    same = lambda s: (row ^ col) < s
    n = -jnp.where(same(2), low, 0.0)
    s = 2
    while s < block:
        e = jnp.where(same(2 * s) & jnp.logical_not(same(s)), low, 0.0)
        x = _mm(n, e, BNN)
        n = n - e - x - _mm(e + x, n, BNN)
        s *= 2
    return n


def _const_spec(shape):
    nd = len(shape)
    return pl.BlockSpec(shape, lambda *_: (0,) * nd, pipeline_mode=pl.Buffered(1))


def _layer_spec(stacked, layer):
    rest = stacked.shape[1:]
    return pl.BlockSpec((None,) + rest, lambda *_: (layer,) + (0,) * len(rest),
                        pipeline_mode=pl.Buffered(1))


def _inproj_kernel(has_vres, *refs):
    if has_vres:
        (x_ref, g_ref, w_ref,
         mu_ref, wup_ref, w0_ref, aup_ref, a0_ref, gup_ref, kk_ref, ka_ref,
         vf_ref, vd_ref, vu_ref, vb_ref,
         r_o, lw_o, k_o, v_o, kk_o, a_o, g_o, pg_ref, pab_ref, p_ref) = refs
    else:
        (x_ref, g_ref, w_ref,
         mu_ref, wup_ref, w0_ref, aup_ref, a0_ref, gup_ref, kk_ref, ka_ref,
         r_o, lw_o, k_o, v_o, kk_o, a_o, g_o, pg_ref, pab_ref, p_ref) = refs
    tm = x_ref.shape[1]
    xb = _rmsnorm(x_ref[0], g_ref[...], NORM_EPS).astype(BF16)
    c_g = RWKV_COLS + GDN_MAIN_COLS

    @pl.when(pl.program_id(1) == 0)
    def _():
        p_ref[:SUBLANES, :] = jnp.zeros((SUBLANES, RWKV_COLS), F32)

    @pl.when(pl.program_id(1) != 0)
    def _():
        p_ref[:SUBLANES, :] = p_ref[tm:, :]

    p = jnp.dot(xb, w_ref[:, :RWKV_COLS], preferred_element_type=F32)
    p_ref[SUBLANES:, :] = p
    p = p + (p_ref[pl.ds(SUBLANES - 1, tm), :] - p) * mu_ref[...]
    w = RWKV_WIDTH
    r, k, v = p[:, :w], p[:, w:2 * w], p[:, 2 * w:3 * w]
    o = 3 * w
    wd = p[:, o:o + DECAY_RANK]
    ad = p[:, o + DECAY_RANK:o + DECAY_RANK + ICLR_RANK]
    gd = p[:, o + DECAY_RANK + ICLR_RANK:]
    w_raw = -jax.nn.softplus(-(w0_ref[...] + _mm(jnp.tanh(wd), wup_ref[...]))) - 0.5
    a = jax.nn.sigmoid(a0_ref[...] + _mm(ad, aup_ref[...]))
    if has_vres:
        mix = jax.nn.sigmoid(vb_ref[...] + _mm(_mm(v, vd_ref[...]), vu_ref[...]))
        v = v + (vf_ref[0] - v) * mix
    r_o[0] = r
    lw_o[0] = -jnp.exp(w_raw)
    k_o[0] = k * (1.0 + (a - 1.0) * ka_ref[...])
    v_o[0] = v
    kk_o[0] = k * kk_ref[...]
    a_o[0] = a
    g_o[0] = _mm(jax.nn.sigmoid(gd), gup_ref[...])
    pg_ref[0] = jnp.dot(xb, w_ref[:, RWKV_COLS:c_g], preferred_element_type=F32)
    pab_ref[0] = jnp.dot(xb, w_ref[:, c_g:], preferred_element_type=F32)


def _inproj(h, g, w_in_all, layer, mu, w_up, w0, a_up, a0, g_up, k_k, k_a, vres):
    b, t, _ = h.shape
    tm = min(ROW_TILE, t)
    has_vres = vres is not None
    seq = lambda n: pl.BlockSpec((1, tm, n), lambda i, j: (i, j, 0))
    ins = [h, g, w_in_all, mu, w_up, w0, a_up, a0, g_up, k_k, k_a]
    specs = ([seq(D_MODEL), _const_spec(g.shape), _layer_spec(w_in_all, layer)]
             + [_const_spec(x.shape) for x in ins[3:]])
    if has_vres:
        v_first, v_down, v_upw, v_bias = vres
        ins += [v_first, v_down, v_upw, v_bias]
        specs += [seq(RWKV_WIDTH)] + [_const_spec(x.shape) for x in (v_down, v_upw, v_bias)]
    out = lambda n: jax.ShapeDtypeStruct((b, t, n), F32)
    return pl.pallas_call(
        functools.partial(_inproj_kernel, has_vres),
        grid=(b, t // tm),
        in_specs=specs,
        out_specs=[seq(RWKV_WIDTH)] * 7 + [seq(GDN_MAIN_COLS), seq(AB_COLS)],
        out_shape=[out(RWKV_WIDTH)] * 7 + [out(GDN_MAIN_COLS), out(AB_COLS)],
        scratch_shapes=[pltpu.VMEM((SUBLANES + tm, RWKV_COLS), F32)],
        compiler_params=pltpu.CompilerParams(
            dimension_semantics=("parallel", "arbitrary"), vmem_limit_bytes=VMEM_LIMIT_BYTES),
        name="inproj",
    )(*ins)


def _stack_heads(x3):
    lo = lax.broadcasted_iota(jnp.int32, x3.shape, 2) < RWKV_HEAD
    zero = jnp.zeros_like(x3)
    return jnp.concatenate([jnp.where(lo, x3, zero), jnp.where(lo, zero, x3)], axis=1)


def _head_sum(x):
    lo = lax.broadcasted_iota(jnp.int32, x.shape, 1) < RWKV_HEAD
    s0 = jnp.sum(jnp.where(lo, x, 0.0), axis=-1, keepdims=True)
    s1 = jnp.sum(jnp.where(lo, 0.0, x), axis=-1, keepdims=True)
    return jnp.where(lo, s0, s1)


def _wkv_kernel(r_ref, lw_ref, k_ref, v_ref, kk_ref, a_ref, gate_ref, rk_ref, lng_ref, lnb_ref,
                o_ref, h_ref):
    @pl.when(pl.program_id(2) == 0)
    def _():
        h_ref[...] = jnp.zeros_like(h_ref)

    nb, tt, _ = r_ref.shape
    flat = lambda ref: ref[...].reshape(nb * tt, LANES)
    r, lw, k, v, kkraw, a = (flat(x) for x in (r_ref, lw_ref, k_ref, v_ref, kk_ref, a_ref))
    ncs = tt // CHUNK
    nc = nb * ncs
    pp = 2 * CHUNK
    kk = kkraw * lax.rsqrt(_head_sum(kkraw * kkraw) + L2_EPS)
    c3 = lambda x: x.reshape(nc, CHUNK, LANES)
    in_chunk = lax.broadcasted_iota(jnp.int32, lw.shape, 0) & (CHUNK - 1)
    gsum = lw
    step = 1
    while step < CHUNK:
        gsum = gsum + jnp.where(in_chunk >= step, pltpu.roll(gsum, step, 0), 0.0)
        step *= 2
    r3, lw3, k3, v3, kk3, b3, g = (c3(x) for x in (r, lw, k, v, kk, kk * a, gsum))
    g_last = g[:, CHUNK - 1:CHUNK, :]
    e_ng = jnp.exp(-g)
    e_tail = jnp.exp(g_last - g)
    rt = r3 * jnp.exp(g)
    kt = (kk3 * jnp.exp(g - lw3)).astype(BF16)
    stack16 = lambda x: _stack_heads(x.astype(BF16))
    kh_s = stack16(k3 * e_ng)
    bh_s = stack16(b3 * e_ng)
    kbar_s = stack16(k3 * e_tail)
    bbar_s = stack16(b3 * e_tail)
    v_s = stack16(v3)

    row = lax.broadcasted_iota(jnp.int32, (CHUNK, LANES), 0)
    tcol = lax.broadcasted_iota(jnp.int32, (CHUNK, LANES), 1) & (CHUNK - 1)
    strict = row > tcol
    incl = row >= tcol
    sc = _mm(jnp.concatenate([kt, rt.astype(BF16)], axis=1),
             jnp.concatenate([kh_s, bh_s], axis=1), BNT)
    a_kk = jnp.where(strict, sc[:, :CHUNK, :LANES], 0.0)
    a_kb = jnp.where(strict, sc[:, :CHUNK, LANES:], 0.0)
    a_rk = jnp.where(incl, sc[:, CHUNK:, :LANES], 0.0)
    a_rb = jnp.where(incl, sc[:, CHUNK:, LANES:], 0.0)

    same = lambda s: (row ^ tcol) < s
    n = -jnp.where(same(2), a_kb, 0.0)
    s = 2
    while s < CHUNK:
        e = jnp.where(same(2 * s) & jnp.logical_not(same(s)), a_kb, 0.0)
        x = _mm(n, stack16(e), BNN)
        n = n - e - x - _mm(e + x, stack16(n), BNN)
        s *= 2

    av = _mm(jnp.concatenate([a_kk, a_rk], axis=1), v_s, BNN)
    akv = av[:, :CHUNK]
    rhs_s = jnp.concatenate([_stack_heads(kt), stack16(akv)], axis=2)
    w_d = kt.astype(F32) + _mm(n, rhs_s[:, :, :LANES], BNN)
    uv_d = akv + _mm(n, rhs_s[:, :, LANES:], BNN)
    wu_s = jnp.concatenate([stack16(w_d), stack16(uv_d)], axis=2)
    corr = _mm(a_rb, wu_s, BNN)
    qp = rt - corr[:, :, :LANES]
    yi = av[:, CHUNK:] - corr[:, :, LANES:]
    bw = _mm(bbar_s, wu_s, BTN)
    mc = bw[:, :, :LANES]
    dmat = _mm(kbar_s, v_s, BTN) - bw[:, :, LANES:]
    row_p, col_p = _tri_masks(pp)
    gcol = jnp.sum(jnp.where(row_p == col_p, jnp.exp(g_last), 0.0), axis=-1, keepdims=True)

    hs = [h_ref[b] for b in range(nb)]
    ys = [None] * nc
    for c in range(ncs):
        for b in range(nb):
            i = b * ncs + c
            z = _mm(jnp.concatenate([qp[i], mc[i]], axis=0), hs[b])
            ys[i] = z[:CHUNK] + yi[i]
            hs[b] = gcol[i] * hs[b] - z[CHUNK:] + dmat[i]
    for b in range(nb):
        h_ref[b] = hs[b]
    y = jnp.concatenate(ys, axis=0)

    inv_n = 1.0 / RWKV_HEAD
    d = y - _head_sum(y) * inv_n
    y = d * lax.rsqrt(_head_sum(d * d) * inv_n + GN_EPS) * lng_ref[...] + lnb_ref[...]
    bonus = _head_sum(r * k * rk_ref[...]) * v
    o_ref[...] = ((y + bonus) * flat(gate_ref)).reshape(nb, tt, LANES)


def _wkv(r, lw, k, v, kk, a, gate, r_k, ln_g, ln_b):
    b, t, w = r.shape
    tt = min(SEQ_TILE, t)
    nb = math.gcd(SEQ_BATCH, b)
    seq = pl.BlockSpec((nb, tt, LANES), lambda i, p, j: (i, j, p))
    par = pl.BlockSpec((1, LANES), lambda i, p, j: (0, p))
    return pl.pallas_call(
        _wkv_kernel,
        grid=(b // nb, w // LANES, t // tt),
        in_specs=[seq] * 7 + [par] * 3,
        out_specs=seq,
        out_shape=jax.ShapeDtypeStruct((b, t, w), F32),
        scratch_shapes=[pltpu.VMEM((nb, LANES, LANES), F32)],
        compiler_params=pltpu.CompilerParams(
            dimension_semantics=("parallel", "parallel", "arbitrary"),
            vmem_limit_bytes=VMEM_LIMIT_BYTES),
        name="wkv7",
    )(r, lw, k, v, kk, a, gate, r_k, ln_g, ln_b)


def _conv_silu(x_ref, prev_ref, w_ref, buf_ref, first):
    nb, tt, _ = x_ref.shape
    x = x_ref[...]
    buf_ref[:, :SUBLANES, :] = jnp.where(first, 0.0, prev_ref[...])
    buf_ref[:, SUBLANES:, :] = x
    acc = x * w_ref[CONV_WIDTH - 1:CONV_WIDTH, :]
    for d in range(1, CONV_WIDTH):
        tap = buf_ref[:, pl.ds(SUBLANES - d, tt), :]
        acc = acc + tap * w_ref[CONV_WIDTH - 1 - d:CONV_WIDTH - d, :]
    return jax.nn.silu(acc).reshape(nb * tt, LANES)


def _col_from_row(rowvec, eye):
    return jnp.sum(jnp.where(eye, rowvec, 0.0), axis=-1, keepdims=True)


def _pick_lane(x, idx):
    lane = lax.broadcasted_iota(jnp.int32, x.shape, x.ndim - 1)
    return jnp.sum(jnp.where(lane == idx, x, 0.0), axis=-1, keepdims=True)


def _gdn_kernel(q_ref, k_ref, v_ref, z_ref, qp_ref, kp_ref, vp_ref, cq_ref, ck_ref, cv_ref,
                abt_ref, alog_ref, dtb_ref, ng_ref, o_ref, s_ref, buf_ref):
    hd = pl.program_id(1)
    first = pl.program_id(2) == 0

    @pl.when(first)
    def _():
        s_ref[...] = jnp.zeros_like(s_ref)

    nb, tt, dk = q_ref.shape
    q = _conv_silu(q_ref, qp_ref, cq_ref, buf_ref.at[0], first)
    k = _conv_silu(k_ref, kp_ref, ck_ref, buf_ref.at[1], first)
    v = _conv_silu(v_ref, vp_ref, cv_ref, buf_ref.at[2], first)
    l2 = lambda x: x * lax.rsqrt(jnp.sum(x * x, axis=-1, keepdims=True) + L2_EPS)
    q = l2(q) * (GDN_HEAD ** -0.5)
    k = l2(k)

    pp = 2 * CHUNK
    nvs = tt // pp
    nv = nb * nvs
    row, col = _tri_masks(pp)
    same = (row ^ col) < CHUNK
    strict = (row > col) & same
    incl = (row >= col) & same
    eye = row == col

    a_log = _pick_lane(alog_ref[...], hd)
    dt_bias = _pick_lane(dtb_ref[...], hd)
    rows_of = lambda i: jnp.concatenate(
        [abt_ref[b, pl.ds(i, 1), :][:, j * pp:(j + 1) * pp] for b in range(nb) for j in range(nvs)],
        axis=0)
    alpha = rows_of(hd)
    beta_r = jax.nn.sigmoid(rows_of(hd + GDN_HEADS))
    g_r = -jnp.exp(a_log) * jax.nn.softplus(alpha + dt_bias)
    upper = ((row <= col) & same).astype(BF16)
    gc_r = _mm_exact_rhs(g_r, upper)
    col3 = lambda x: jnp.stack([_col_from_row(x[j:j + 1, :], eye) for j in range(nv)], axis=0)
    gc = col3(gc_r)
    beta3 = col3(beta_r)
    top = lax.broadcasted_iota(jnp.int32, (pp, 1), 0) < CHUNK
    g_last = jnp.where(top, gc[:, CHUNK - 1:CHUNK, :], gc[:, pp - 1:pp, :])
    decay = jnp.where(incl, jnp.exp(jnp.minimum(gc - gc_r[:, None, :], 0.0)), 0.0)

    c3 = lambda x: x.reshape(nv, pp, x.shape[-1])
    q3, k3, v3 = (c3(x) for x in (q, k, v))
    kb = k3 * beta3
    sc = _mm(jnp.concatenate([kb, q3], axis=1), k3, BNT)
    low = jnp.where(strict, sc[:, :pp] * decay, 0.0)
    a_qk = sc[:, pp:] * decay
    n = _unit_lower_inverse_minus_eye(low, row, col, CHUNK)
    e_gc = jnp.exp(gc)
    rhs = jnp.concatenate([kb * e_gc, v3 * beta3], axis=2)
    wu = rhs + _mm(n, rhs, BNN)
    corr = _mm(a_qk, wu, BNN)
    qp = q3 * e_gc - corr[:, :, :dk]
    yi = corr[:, :, dk:]
    kd = k3 * jnp.exp(g_last - gc)
    e_last = jnp.exp(g_last)

    per_chunk = lambda x: x.reshape(2 * nv, CHUNK, x.shape[-1])
    md = _mm(per_chunk(kd), per_chunk(wu), BTN)
    lhs = jnp.concatenate([per_chunk(qp), md[:, :, :dk]], axis=1).astype(BF16)
    yi2 = per_chunk(yi)
    e_last2 = per_chunk(e_last)[:, 0:1, :]

    ss = [s_ref[b] for b in range(nb)]
    outs = [None] * (2 * nv)
    for c in range(2 * nvs):
        for b in range(nb):
            i = b * 2 * nvs + c
            z = _mm(lhs[i], ss[b])
            outs[i] = z[:CHUNK] + yi2[i]
            ss[b] = e_last2[i] * ss[b] - z[CHUNK:] + md[i, :, dk:]
    for b in range(nb):
        s_ref[b] = ss[b]
    o = jnp.concatenate(outs, axis=0)
    o = o * lax.rsqrt(jnp.mean(o * o, axis=-1, keepdims=True) + GATED_NORM_EPS)
    gate = jax.nn.silu(z_ref[...].reshape(nb * tt, dk))
    o_ref[...] = (o * ng_ref[...] * gate).reshape(nb, tt, dk)


def _gdn(p_g, p_abt, conv_w, a_log8, dt_bias8, norm_g):
    b, t, _ = p_g.shape
    tt = min(SEQ_TILE, t)
    nb = math.gcd(SEQ_BATCH, b)
    nh = GDN_HEADS
    seq = lambda off: pl.BlockSpec((nb, tt, LANES), lambda i, h, j: (i, j, off + h))
    prev = lambda off: pl.BlockSpec(
        (nb, SUBLANES, LANES),
        lambda i, h, j: (i, jnp.maximum(j * (tt // SUBLANES) - 1, 0), off + h))
    cw = lambda off: pl.BlockSpec((CONV_WIDTH, LANES), lambda i, h, j: (0, off + h))
    small = lambda n: pl.BlockSpec((1, n), lambda i, h, j: (0, 0))
    return pl.pallas_call(
        _gdn_kernel,
        grid=(b // nb, nh, t // tt),
        in_specs=[seq(0), seq(nh), seq(2 * nh), seq(3 * nh), prev(0), prev(nh), prev(2 * nh),
                  cw(0), cw(nh), cw(2 * nh),
                  pl.BlockSpec((nb, AB_COLS, tt), lambda i, h, j: (i, 0, j)),
                  small(AB_COLS), small(AB_COLS), small(LANES)],
        out_specs=pl.BlockSpec((nb, tt, LANES), lambda i, h, j: (i, j, h)),
        out_shape=jax.ShapeDtypeStruct((b, t, GDN_WIDTH), F32),
        scratch_shapes=[pltpu.VMEM((nb, GDN_HEAD, GDN_HEAD), F32),
                        pltpu.VMEM((3, nb, SUBLANES + tt, LANES), F32)],
        compiler_params=pltpu.CompilerParams(
            dimension_semantics=("parallel", "parallel", "arbitrary"),
            vmem_limit_bytes=VMEM_LIMIT_BYTES),
        name="gdn",
    )(p_g, p_g, p_g, p_g, p_g, p_g, p_g, conv_w, conv_w, conv_w, p_abt, a_log8, dt_bias8, norm_g)


def _out_mlp_kernel(final, ya_ref, yb_ref, h_ref, wo_ref, g2_ref, up_ref, dn_ref, gf_ref, o_ref):
    w = RWKV_WIDTH
    h = h_ref[...]
    h = h + jnp.dot(ya_ref[...].astype(BF16), wo_ref[:w, :], preferred_element_type=F32)
    h = h + jnp.dot(yb_ref[...].astype(BF16), wo_ref[w:, :], preferred_element_type=F32)
    xb = _rmsnorm(h, g2_ref[...], NORM_EPS).astype(BF16)
    hid = jnp.dot(xb, up_ref[...], preferred_element_type=F32)
    hid = jnp.square(jnp.maximum(hid, 0.0)).astype(BF16)
    acc = h + jnp.dot(hid, dn_ref[...], preferred_element_type=F32)
    if final:
        acc = _rmsnorm(acc, gf_ref[...], NORM_EPS)
    o_ref[...] = acc


def _out_mlp(ya, yb, h2d, w_out, g2, up, dn, gf, layer, final):
    m = h2d.shape[0]
    tm = min(ROW_TILE, m)
    row = lambda n: pl.BlockSpec((tm, n), lambda i: (i, 0))
    return pl.pallas_call(
        functools.partial(_out_mlp_kernel, final),
        grid=(m // tm,),
        in_specs=[row(RWKV_WIDTH), row(GDN_WIDTH), row(D_MODEL), _layer_spec(w_out, layer),
                  _const_spec(g2.shape), _layer_spec(up, layer), _layer_spec(dn, layer),
                  _const_spec(gf.shape)],
        out_specs=row(D_MODEL),
        out_shape=jax.ShapeDtypeStruct((m, D_MODEL), F32),
        compiler_params=pltpu.CompilerParams(
            dimension_semantics=("parallel",), vmem_limit_bytes=VMEM_LIMIT_BYTES),
        name="out_mlp",
    )(ya, yb, h2d, w_out, g2, up, dn, gf)


def kernel(x, norm1_g, w_in, shift_mu, rw_w_up, rw_w0, rw_a_up, rw_a0, rw_g_up, rw_k_k, rw_k_a,
           rw_r_k, rw_ln_g, rw_ln_b, rw_vres_down, rw_vres_up, rw_vres_b, gdn_conv, gdn_A_log,
           gdn_dt_bias, gdn_norm_g, w_out, norm2_g, mlp_up, mlp_down, final_g):
    b, t, d = x.shape
    depth = w_in.shape[0]
    m = b * t
    row = lambda v: v.reshape(1, -1)
    pad8 = lambda v: jnp.pad(v, (0, AB_COLS - v.shape[0])).reshape(1, AB_COLS)

    w_in_b, w_out_b, up_b, dn_b = (w.astype(BF16) for w in (w_in, w_out, mlp_up, mlp_down))

    h = x.reshape(m, d)
    v_first = None
    for l in range(depth):
        vres = None
        if l > 0:
            vres = (v_first, rw_vres_down[l - 1].astype(BF16), rw_vres_up[l - 1].astype(BF16),
                    row(rw_vres_b[l - 1]))
        r, lw, k, v, kk, a, gate, p_g, p_ab = _inproj(
            h.reshape(b, t, d), row(norm1_g[l]), w_in_b, l,
            row(shift_mu[l]), rw_w_up[l].astype(BF16), row(rw_w0[l]),
            rw_a_up[l].astype(BF16), row(rw_a0[l]), rw_g_up[l].astype(BF16),
            row(rw_k_k[l]), row(rw_k_a[l]), vres)
        if l == 0:
            v_first = v
        p_abt = jnp.swapaxes(p_ab, 1, 2)
        y_a = _wkv(r, lw, k, v, kk, a, gate, row(rw_r_k[l]), row(rw_ln_g[l]), row(rw_ln_b[l]))
        y_b = _gdn(p_g, p_abt, gdn_conv[l], pad8(gdn_A_log[l]), pad8(gdn_dt_bias[l]),
                   row(gdn_norm_g[l]))
        h = _out_mlp(y_a.reshape(m, RWKV_WIDTH), y_b.reshape(m, GDN_WIDTH), h,
                     w_out_b, row(norm2_g[l]), up_b, dn_b, row(final_g), l, l == depth - 1)
    return h.reshape(b, t, d)
```

```python
import functools
import math

import jax
import jax.numpy as jnp
from jax import lax
from jax.experimental import pallas as pl
from jax.experimental.pallas import tpu as pltpu

F32 = jnp.float32
BF16 = jnp.bfloat16

D_MODEL = 1024
RWKV_WIDTH = 512
RWKV_HEAD = 64
DECAY_RANK = 64
ICLR_RANK = 64
GATE_RANK = 128
GN_EPS = 64e-5
GDN_WIDTH = 512
GDN_HEAD = 128
GDN_HEADS = 4
CONV_WIDTH = 4
CHUNK = 64
D_FF = 4 * D_MODEL
NORM_EPS = 1e-5
GATED_NORM_EPS = 1e-6
L2_EPS = 1e-6
RWKV_COLS = 3 * RWKV_WIDTH + DECAY_RANK + ICLR_RANK + GATE_RANK
GDN_MAIN_COLS = 4 * GDN_WIDTH
AB_COLS = 2 * GDN_HEADS

LANES = 128
SUBLANES = 8
VMEM_LIMIT_BYTES = 56 * 1024 * 1024

ROW_TILE = 512
SEQ_TILE = 256
SEQ_BATCH = 8

NN = (((1,), (0,)), ((), ()))
NT = (((1,), (1,)), ((), ()))
BNN = (((2,), (1,)), ((0,), (0,)))
BNT = (((2,), (2,)), ((0,), (0,)))
BTN = (((1,), (1,)), ((0,), (0,)))
TN = (((0,), (0,)), ((), ()))


def _mm(a, b, dims=NN):
    return lax.dot_general(a.astype(BF16), b.astype(BF16), dims, preferred_element_type=F32)


def _mm_exact_rhs(a, b_bf16, dims=NN):
    a1 = a.astype(BF16)
    r1 = a - a1.astype(F32)
    a2 = r1.astype(BF16)
    a3 = (r1 - a2.astype(F32)).astype(BF16)
    d = lambda y: lax.dot_general(y, b_bf16, dims, preferred_element_type=F32)
    return d(a1) + (d(a2) + d(a3))


def _rmsnorm(x, g, eps):
    return x * lax.rsqrt(jnp.mean(x * x, axis=-1, keepdims=True) + eps) * g


def _tri_masks(c):
    row = lax.broadcasted_iota(jnp.int32, (c, c), 0)
    col = lax.broadcasted_iota(jnp.int32, (c, c), 1)
    return row, col


def _unit_lower_inverse_minus_eye(low, row, col, block):
    same = lambda s: (row ^ col) < s
    n = -jnp.where(same(2), low, 0.0)
    s = 2
    while s < block:
        e = jnp.where(same(2 * s) & jnp.logical_not(same(s)), low, 0.0)
        x = _mm(n, e, BNN)
        n = n - e - x - _mm(e + x, n, BNN)
        s *= 2
    return n


def _const_spec(shape):
    nd = len(shape)
    return pl.BlockSpec(shape, lambda *_: (0,) * nd, pipeline_mode=pl.Buffered(1))


def _layer_spec(stacked, layer):
    rest = stacked.shape[1:]
    return pl.BlockSpec((None,) + rest, lambda *_: (layer,) + (0,) * len(rest),
                        pipeline_mode=pl.Buffered(1))


def _inproj_kernel(has_vres, *refs):
    if has_vres:
        (x_ref, g_ref, w_ref,
         mu_ref, wup_ref, w0_ref, aup_ref, a0_ref, gup_ref, kk_ref, ka_ref,
         vf_ref, vd_ref, vu_ref, vb_ref,
         r_o, lw_o, k_o, v_o, kk_o, a_o, g_o, pg_ref, pab_ref, p_ref) = refs
    else:
        (x_ref, g_ref, w_ref,
         mu_ref, wup_ref, w0_ref, aup_ref, a0_ref, gup_ref, kk_ref, ka_ref,
         r_o, lw_o, k_o, v_o, kk_o, a_o, g_o, pg_ref, pab_ref, p_ref) = refs
    tm = x_ref.shape[1]
    xb = _rmsnorm(x_ref[0], g_ref[...], NORM_EPS).astype(BF16)
    c_g = RWKV_COLS + GDN_MAIN_COLS

    @pl.when(pl.program_id(1) == 0)
    def _():
        p_ref[:SUBLANES, :] = jnp.zeros((SUBLANES, RWKV_COLS), F32)

    @pl.when(pl.program_id(1) != 0)
    def _():
        p_ref[:SUBLANES, :] = p_ref[tm:, :]

    p = jnp.dot(xb, w_ref[:, :RWKV_COLS], preferred_element_type=F32)
    p_ref[SUBLANES:, :] = p
    p = p + (p_ref[pl.ds(SUBLANES - 1, tm), :] - p) * mu_ref[...]
    w = RWKV_WIDTH
    r, k, v = p[:, :w], p[:, w:2 * w], p[:, 2 * w:3 * w]
    o = 3 * w
    wd = p[:, o:o + DECAY_RANK]
    ad = p[:, o + DECAY_RANK:o + DECAY_RANK + ICLR_RANK]
    gd = p[:, o + DECAY_RANK + ICLR_RANK:]
    w_raw = -jax.nn.softplus(-(w0_ref[...] + _mm(jnp.tanh(wd), wup_ref[...]))) - 0.5
    a = jax.nn.sigmoid(a0_ref[...] + _mm(ad, aup_ref[...]))
    if has_vres:
        mix = jax.nn.sigmoid(vb_ref[...] + _mm(_mm(v, vd_ref[...]), vu_ref[...]))
        v = v + (vf_ref[0] - v) * mix
    r_o[0] = r
    lw_o[0] = -jnp.exp(w_raw)
    k_o[0] = k * (1.0 + (a - 1.0) * ka_ref[...])
    v_o[0] = v
    kk_o[0] = k * kk_ref[...]
    a_o[0] = a
    g_o[0] = _mm(jax.nn.sigmoid(gd), gup_ref[...])
    pg_ref[0] = jnp.dot(xb, w_ref[:, RWKV_COLS:c_g], preferred_element_type=F32)
    pab_ref[0] = jnp.dot(xb, w_ref[:, c_g:], preferred_element_type=F32)


def _inproj(h, g, w_in_all, layer, mu, w_up, w0, a_up, a0, g_up, k_k, k_a, vres):
    b, t, _ = h.shape
    tm = min(ROW_TILE, t)
    has_vres = vres is not None
    seq = lambda n: pl.BlockSpec((1, tm, n), lambda i, j: (i, j, 0))
    ins = [h, g, w_in_all, mu, w_up, w0, a_up, a0, g_up, k_k, k_a]
    specs = ([seq(D_MODEL), _const_spec(g.shape), _layer_spec(w_in_all, layer)]
             + [_const_spec(x.shape) for x in ins[3:]])
    if has_vres:
        v_first, v_down, v_upw, v_bias = vres
        ins += [v_first, v_down, v_upw, v_bias]
        specs += [seq(RWKV_WIDTH)] + [_const_spec(x.shape) for x in (v_down, v_upw, v_bias)]
    out = lambda n: jax.ShapeDtypeStruct((b, t, n), F32)
    return pl.pallas_call(
        functools.partial(_inproj_kernel, has_vres),
        grid=(b, t // tm),
        in_specs=specs,
        out_specs=[seq(RWKV_WIDTH)] * 7 + [seq(GDN_MAIN_COLS), seq(AB_COLS)],
        out_shape=[out(RWKV_WIDTH)] * 7 + [out(GDN_MAIN_COLS), out(AB_COLS)],
        scratch_shapes=[pltpu.VMEM((SUBLANES + tm, RWKV_COLS), F32)],
        compiler_params=pltpu.CompilerParams(
            dimension_semantics=("parallel", "arbitrary"), vmem_limit_bytes=VMEM_LIMIT_BYTES),
        name="inproj",
    )(*ins)


def _stack_heads(x3):
    lo = lax.broadcasted_iota(jnp.int32, x3.shape, 2) < RWKV_HEAD
    zero = jnp.zeros_like(x3)
    return jnp.concatenate([jnp.where(lo, x3, zero), jnp.where(lo, zero, x3)], axis=1)


def _head_sum(x):
    lo = lax.broadcasted_iota(jnp.int32, x.shape, 1) < RWKV_HEAD
    s0 = jnp.sum(jnp.where(lo, x, 0.0), axis=-1, keepdims=True)
    s1 = jnp.sum(jnp.where(lo, 0.0, x), axis=-1, keepdims=True)
    return jnp.where(lo, s0, s1)


def _wkv_kernel(r_ref, lw_ref, k_ref, v_ref, kk_ref, a_ref, gate_ref, rk_ref, lng_ref, lnb_ref,
                o_ref, h_ref):
    @pl.when(pl.program_id(2) == 0)
    def _():
        h_ref[...] = jnp.zeros_like(h_ref)

    nb, tt, _ = r_ref.shape
    flat = lambda ref: ref[...].reshape(nb * tt, LANES)
    r, lw, k, v, kkraw, a = (flat(x) for x in (r_ref, lw_ref, k_ref, v_ref, kk_ref, a_ref))
    ncs = tt // CHUNK
    nc = nb * ncs
    pp = 2 * CHUNK
    kk = kkraw * lax.rsqrt(_head_sum(kkraw * kkraw) + L2_EPS)
    c3 = lambda x: x.reshape(nc, CHUNK, LANES)
    in_chunk = lax.broadcasted_iota(jnp.int32, lw.shape, 0) & (CHUNK - 1)
    gsum = lw
    step = 1
    while step < CHUNK:
        gsum = gsum + jnp.where(in_chunk >= step, pltpu.roll(gsum, step, 0), 0.0)
        step *= 2
    r3, lw3, k3, v3, kk3, b3, g = (c3(x) for x in (r, lw, k, v, kk, kk * a, gsum))
    g_last = g[:, CHUNK - 1:CHUNK, :]
    e_ng = jnp.exp(-g)
    e_tail = jnp.exp(g_last - g)
    rt = r3 * jnp.exp(g)
    kt = (kk3 * jnp.exp(g - lw3)).astype(BF16)
    stack16 = lambda x: _stack_heads(x.astype(BF16))
    kh_s = stack16(k3 * e_ng)
    bh_s = stack16(b3 * e_ng)
    kbar_s = stack16(k3 * e_tail)
    bbar_s = stack16(b3 * e_tail)
    v_s = stack16(v3)

    row = lax.broadcasted_iota(jnp.int32, (CHUNK, LANES), 0)
    tcol = lax.broadcasted_iota(jnp.int32, (CHUNK, LANES), 1) & (CHUNK - 1)
    strict = row > tcol
    incl = row >= tcol
    sc = _mm(jnp.concatenate([kt, rt.astype(BF16)], axis=1),
             jnp.concatenate([kh_s, bh_s], axis=1), BNT)
    a_kk = jnp.where(strict, sc[:, :CHUNK, :LANES], 0.0)
    a_kb = jnp.where(strict, sc[:, :CHUNK, LANES:], 0.0)
    a_rk = jnp.where(incl, sc[:, CHUNK:, :LANES], 0.0)
    a_rb = jnp.where(incl, sc[:, CHUNK:, LANES:], 0.0)

    same = lambda s: (row ^ tcol) < s
    n = -jnp.where(same(2), a_kb, 0.0)
    s = 2
    while s < CHUNK:
        e = jnp.where(same(2 * s) & jnp.logical_not(same(s)), a_kb, 0.0)
        x = _mm(n, stack16(e), BNN)
        n = n - e - x - _mm(e + x, stack16(n), BNN)
        s *= 2

    av = _mm(jnp.concatenate([a_kk, a_rk], axis=1), v_s, BNN)
    akv = av[:, :CHUNK]
    rhs_s = jnp.concatenate([_stack_heads(kt), stack16(akv)], axis=2)
    wu_d = jnp.concatenate([kt.astype(F32), akv], axis=2) + _mm(n, rhs_s, BNN)
    wu_s = jnp.concatenate([stack16(wu_d[:, :, :LANES]), stack16(wu_d[:, :, LANES:])], axis=2)
    corr = _mm(a_rb, wu_s, BNN)
    qp = rt - corr[:, :, :LANES]
    yi = av[:, CHUNK:] - corr[:, :, LANES:]
    bw = _mm(bbar_s, wu_s, BTN)
    mc = bw[:, :, :LANES]
    dmat = _mm(kbar_s, v_s, BTN) - bw[:, :, LANES:]
    row_p, col_p = _tri_masks(pp)
    gcol = jnp.sum(jnp.where(row_p == col_p, jnp.exp(g_last), 0.0), axis=-1, keepdims=True)

    hs = [h_ref[b] for b in range(nb)]
    ys = [None] * nc
    for c in range(ncs):
        for b in range(nb):
            i = b * ncs + c
            z = _mm(jnp.concatenate([qp[i], mc[i]], axis=0), hs[b])
            ys[i] = z[:CHUNK] + yi[i]
            hs[b] = gcol[i] * hs[b] - z[CHUNK:] + dmat[i]
    for b in range(nb):
        h_ref[b] = hs[b]
    y = jnp.concatenate(ys, axis=0)

    inv_n = 1.0 / RWKV_HEAD
    d = y - _head_sum(y) * inv_n
    y = d * lax.rsqrt(_head_sum(d * d) * inv_n + GN_EPS) * lng_ref[...] + lnb_ref[...]
    bonus = _head_sum(r * k * rk_ref[...]) * v
    o_ref[...] = ((y + bonus) * flat(gate_ref)).reshape(nb, tt, LANES)


def _wkv(r, lw, k, v, kk, a, gate, r_k, ln_g, ln_b):
    b, t, w = r.shape
    tt = min(SEQ_TILE, t)
    nb = math.gcd(SEQ_BATCH, b)
    seq = pl.BlockSpec((nb, tt, LANES), lambda i, p, j: (i, j, p))
    par = pl.BlockSpec((1, LANES), lambda i, p, j: (0, p))
    return pl.pallas_call(
        _wkv_kernel,
        grid=(b // nb, w // LANES, t // tt),
        in_specs=[seq] * 7 + [par] * 3,
        out_specs=seq,
        out_shape=jax.ShapeDtypeStruct((b, t, w), F32),
        scratch_shapes=[pltpu.VMEM((nb, LANES, LANES), F32)],
        compiler_params=pltpu.CompilerParams(
            dimension_semantics=("parallel", "parallel", "arbitrary"),
            vmem_limit_bytes=VMEM_LIMIT_BYTES),
        name="wkv7",
    )(r, lw, k, v, kk, a, gate, r_k, ln_g, ln_b)


def _conv_silu(x_ref, prev_ref, w_ref, buf_ref, first):
    nb, tt, _ = x_ref.shape
    x = x_ref[...]
    buf_ref[:, :SUBLANES, :] = jnp.where(first, 0.0, prev_ref[...])
    buf_ref[:, SUBLANES:, :] = x
    acc = x * w_ref[CONV_WIDTH - 1:CONV_WIDTH, :]
    for d in range(1, CONV_WIDTH):
        tap = buf_ref[:, pl.ds(SUBLANES - d, tt), :]
        acc = acc + tap * w_ref[CONV_WIDTH - 1 - d:CONV_WIDTH - d, :]
    return jax.nn.silu(acc).reshape(nb * tt, LANES)


def _col_from_row(rowvec, eye):
    return jnp.sum(jnp.where(eye, rowvec, 0.0), axis=-1, keepdims=True)


def _pick_lane(x, idx):
    lane = lax.broadcasted_iota(jnp.int32, x.shape, x.ndim - 1)
    return jnp.sum(jnp.where(lane == idx, x, 0.0), axis=-1, keepdims=True)


def _gdn_kernel(q_ref, k_ref, v_ref, z_ref, qp_ref, kp_ref, vp_ref, cq_ref, ck_ref, cv_ref,
                abt_ref, alog_ref, dtb_ref, ng_ref, o_ref, s_ref, buf_ref):
    hd = pl.program_id(1)
    first = pl.program_id(2) == 0

    @pl.when(first)
    def _():
        s_ref[...] = jnp.zeros_like(s_ref)

    nb, tt, dk = q_ref.shape
    q = _conv_silu(q_ref, qp_ref, cq_ref, buf_ref.at[0], first)
    k = _conv_silu(k_ref, kp_ref, ck_ref, buf_ref.at[1], first)
    v = _conv_silu(v_ref, vp_ref, cv_ref, buf_ref.at[2], first)
    l2 = lambda x: x * lax.rsqrt(jnp.sum(x * x, axis=-1, keepdims=True) + L2_EPS)
    q = l2(q) * (GDN_HEAD ** -0.5)
    k = l2(k)

    pp = 2 * CHUNK
    nvs = tt // pp
    nv = nb * nvs
    row, col = _tri_masks(pp)
    same = (row ^ col) < CHUNK
    strict = (row > col) & same
    incl = (row >= col) & same
    eye = row == col

    a_log = _pick_lane(alog_ref[...], hd)
    dt_bias = _pick_lane(dtb_ref[...], hd)
    rows_of = lambda i: jnp.concatenate(
        [abt_ref[b, pl.ds(i, 1), :][:, j * pp:(j + 1) * pp] for b in range(nb) for j in range(nvs)],
        axis=0)
    alpha = rows_of(hd)
    beta_r = jax.nn.sigmoid(rows_of(hd + GDN_HEADS))
    g_r = -jnp.exp(a_log) * jax.nn.softplus(alpha + dt_bias)
    upper = ((row <= col) & same).astype(BF16)
    gc_r = _mm_exact_rhs(g_r, upper)
    col3 = lambda x: jnp.stack([_col_from_row(x[j:j + 1, :], eye) for j in range(nv)], axis=0)
    gc = col3(gc_r)
    beta3 = col3(beta_r)
    top = lax.broadcasted_iota(jnp.int32, (pp, 1), 0) < CHUNK
    g_last = jnp.where(top, gc[:, CHUNK - 1:CHUNK, :], gc[:, pp - 1:pp, :])
    decay = jnp.where(incl, jnp.exp(jnp.minimum(gc - gc_r[:, None, :], 0.0)), 0.0)

    c3 = lambda x: x.reshape(nv, pp, x.shape[-1])
    q3, k3, v3 = (c3(x) for x in (q, k, v))
    kb = k3 * beta3
    sc = _mm(jnp.concatenate([kb, q3], axis=1), k3, BNT)
    low = jnp.where(strict, sc[:, :pp] * decay, 0.0)
    a_qk = sc[:, pp:] * decay
    n = _unit_lower_inverse_minus_eye(low, row, col, CHUNK)
    e_gc = jnp.exp(gc)
    rhs = jnp.concatenate([kb * e_gc, v3 * beta3], axis=2)
    wu = rhs + _mm(n, rhs, BNN)
    corr = _mm(a_qk, wu, BNN)
    qp = q3 * e_gc - corr[:, :, :dk]
    yi = corr[:, :, dk:]
    kd = k3 * jnp.exp(g_last - gc)
    e_last = jnp.exp(g_last)

    per_chunk = lambda x: x.reshape(2 * nv, CHUNK, x.shape[-1])
    md = _mm(per_chunk(kd), per_chunk(wu), BTN)
    lhs = jnp.concatenate([per_chunk(qp), md[:, :, :dk]], axis=1).astype(BF16)
    yi2 = per_chunk(yi)
    e_last2 = per_chunk(e_last)[:, 0:1, :]

    ss = [s_ref[b] for b in range(nb)]
    outs = [None] * (2 * nv)
    for c in range(2 * nvs):
        for b in range(nb):
            i = b * 2 * nvs + c
            z = _mm(lhs[i], ss[b])
            outs[i] = z[:CHUNK] + yi2[i]
            ss[b] = e_last2[i] * ss[b] - z[CHUNK:] + md[i, :, dk:]
    for b in range(nb):
        s_ref[b] = ss[b]
    o = jnp.concatenate(outs, axis=0)
    o = o * lax.rsqrt(jnp.mean(o * o, axis=-1, keepdims=True) + GATED_NORM_EPS)
    gate = jax.nn.silu(z_ref[...].reshape(nb * tt, dk))
    o_ref[...] = (o * ng_ref[...] * gate).reshape(nb, tt, dk)


def _gdn(p_g, p_abt, conv_w, a_log8, dt_bias8, norm_g):
    b, t, _ = p_g.shape
    tt = min(SEQ_TILE, t)
    nb = math.gcd(SEQ_BATCH, b)
    nh = GDN_HEADS
    seq = lambda off: pl.BlockSpec((nb, tt, LANES), lambda i, h, j: (i, j, off + h))
    prev = lambda off: pl.BlockSpec(
        (nb, SUBLANES, LANES),
        lambda i, h, j: (i, jnp.maximum(j * (tt // SUBLANES) - 1, 0), off + h))
    cw = lambda off: pl.BlockSpec((CONV_WIDTH, LANES), lambda i, h, j: (0, off + h))
    small = lambda n: pl.BlockSpec((1, n), lambda i, h, j: (0, 0))
    return pl.pallas_call(
        _gdn_kernel,
        grid=(b // nb, nh, t // tt),
        in_specs=[seq(0), seq(nh), seq(2 * nh), seq(3 * nh), prev(0), prev(nh), prev(2 * nh),
                  cw(0), cw(nh), cw(2 * nh),
                  pl.BlockSpec((nb, AB_COLS, tt), lambda i, h, j: (i, 0, j)),
                  small(AB_COLS), small(AB_COLS), small(LANES)],
        out_specs=pl.BlockSpec((nb, tt, LANES), lambda i, h, j: (i, j, h)),
        out_shape=jax.ShapeDtypeStruct((b, t, GDN_WIDTH), F32),
        scratch_shapes=[pltpu.VMEM((nb, GDN_HEAD, GDN_HEAD), F32),
                        pltpu.VMEM((3, nb, SUBLANES + tt, LANES), F32)],
        compiler_params=pltpu.CompilerParams(
            dimension_semantics=("parallel", "parallel", "arbitrary"),
            vmem_limit_bytes=VMEM_LIMIT_BYTES),
        name="gdn",
    )(p_g, p_g, p_g, p_g, p_g, p_g, p_g, conv_w, conv_w, conv_w, p_abt, a_log8, dt_bias8, norm_g)


def _out_mlp_kernel(final, ya_ref, yb_ref, h_ref, wo_ref, g2_ref, up_ref, dn_ref, gf_ref, o_ref):
    w = RWKV_WIDTH
    h = h_ref[...]
    h = h + jnp.dot(ya_ref[...].astype(BF16), wo_ref[:w, :], preferred_element_type=F32)
    h = h + jnp.dot(yb_ref[...].astype(BF16), wo_ref[w:, :], preferred_element_type=F32)
    xb = _rmsnorm(h, g2_ref[...], NORM_EPS).astype(BF16)
    hid = jnp.dot(xb, up_ref[...], preferred_element_type=F32)
    hid = jnp.square(jnp.maximum(hid, 0.0)).astype(BF16)
    acc = h + jnp.dot(hid, dn_ref[...], preferred_element_type=F32)
    if final:
        acc = _rmsnorm(acc, gf_ref[...], NORM_EPS)
    o_ref[...] = acc


def _out_mlp(ya, yb, h2d, w_out, g2, up, dn, gf, layer, final):
    m = h2d.shape[0]
    tm = min(ROW_TILE, m)
    row = lambda n: pl.BlockSpec((tm, n), lambda i: (i, 0))
    return pl.pallas_call(
        functools.partial(_out_mlp_kernel, final),
        grid=(m // tm,),
        in_specs=[row(RWKV_WIDTH), row(GDN_WIDTH), row(D_MODEL), _layer_spec(w_out, layer),
                  _const_spec(g2.shape), _layer_spec(up, layer), _layer_spec(dn, layer),
                  _const_spec(gf.shape)],
        out_specs=row(D_MODEL),
        out_shape=jax.ShapeDtypeStruct((m, D_MODEL), F32),
        compiler_params=pltpu.CompilerParams(
            dimension_semantics=("parallel",), vmem_limit_bytes=VMEM_LIMIT_BYTES),
        name="out_mlp",
    )(ya, yb, h2d, w_out, g2, up, dn, gf)


def kernel(x, norm1_g, w_in, shift_mu, rw_w_up, rw_w0, rw_a_up, rw_a0, rw_g_up, rw_k_k, rw_k_a,
           rw_r_k, rw_ln_g, rw_ln_b, rw_vres_down, rw_vres_up, rw_vres_b, gdn_conv, gdn_A_log,
           gdn_dt_bias, gdn_norm_g, w_out, norm2_g, mlp_up, mlp_down, final_g):
    b, t, d = x.shape
    depth = w_in.shape[0]
    m = b * t
    row = lambda v: v.reshape(1, -1)
    pad8 = lambda v: jnp.pad(v, (0, AB_COLS - v.shape[0])).reshape(1, AB_COLS)

    w_in_b, w_out_b, up_b, dn_b = (w.astype(BF16) for w in (w_in, w_out, mlp_up, mlp_down))

    h = x.reshape(m, d)
    v_first = None
    for l in range(depth):
        vres = None
        if l > 0:
            vres = (v_first, rw_vres_down[l - 1].astype(BF16), rw_vres_up[l - 1].astype(BF16),
                    row(rw_vres_b[l - 1]))
        r, lw, k, v, kk, a, gate, p_g, p_ab = _inproj(
            h.reshape(b, t, d), row(norm1_g[l]), w_in_b, l,
            row(shift_mu[l]), rw_w_up[l].astype(BF16), row(rw_w0[l]),
            rw_a_up[l].astype(BF16), row(rw_a0[l]), rw_g_up[l].astype(BF16),
            row(rw_k_k[l]), row(rw_k_a[l]), vres)
        if l == 0:
            v_first = v
        p_abt = jnp.swapaxes(p_ab, 1, 2)
        y_a = _wkv(r, lw, k, v, kk, a, gate, row(rw_r_k[l]), row(rw_ln_g[l]), row(rw_ln_b[l]))
        y_b = _gdn(p_g, p_abt, gdn_conv[l], pad8(gdn_A_log[l]), pad8(gdn_dt_bias[l]),
                   row(gdn_norm_g[l]))
        h = _out_mlp(y_a.reshape(m, RWKV_WIDTH), y_b.reshape(m, GDN_WIDTH), h,
                     w_out_b, row(norm2_g[l]), up_b, dn_b, row(final_g), l, l == depth - 1)
    return h.reshape(b, t, d)
```
